```python
import math
import jax, jax.numpy as jnp
from jax import lax
import numpy as np


D_MODEL = 1024
BATCH = 8
SEQ = 2048
DEPTH = 2
DEC_BATCH = 128
DEC_SEQ = 8
PAST_LEN = 2048
PAGE_SIZE = 128

N_EVEN = (DEPTH + 1) // 2
N_ODD = DEPTH // 2
D_MIX = D_MODEL
H_A = 8
D_HEAD_A = 64
D_A = H_A * D_HEAD_A
Q_BLOCK = 128
SB_BIAS_INIT = -8.0
POOL_WINDOWS = (2, 4, 8, 16)
G_B = len(POOL_WINDOWS)
D_B = D_MIX - D_A
C_B = D_B // G_B
POOL_BUF = max(POOL_WINDOWS) - 1
D_IN_EVEN = 3 * D_A + D_B
H_C = 8
D_K = 128
D_V = 128
CONV_W = 4
D_CONV = H_C * (2 * D_K + D_V)
D_IN_ODD = D_CONV + H_C * D_V + 2 * H_C
CHUNK = 64
D_FF = 2816
FFN_HALF = 0.5
EPS = 1e-6

kernel_name = 'hybrid_stickbreak_pool_gdn_macaron_step'


def rmsnorm(x, g):
    xf = x.astype(jnp.float32)
    y = xf * lax.rsqrt(jnp.mean(xf * xf, axis=-1, keepdims=True) + EPS)
    return (y * g.astype(jnp.float32)).astype(x.dtype)


def l2norm(x):
    xf = x.astype(jnp.float32)
    return xf * lax.rsqrt(jnp.sum(xf * xf, axis=-1, keepdims=True) + EPS)


def swiglu(h, w_in, w_out):
    gate, up = jnp.split(h @ w_in, 2, axis=-1)
    return (jax.nn.silu(gate) * up) @ w_out


def stick_breaking_attention(q, k, v, sb_bias, q_offset):
    B, T, H, d = q.shape
    S = k.shape[1]
    qb = Q_BLOCK if T % Q_BLOCK == 0 else T
    nb = T // qb
    k_pos = jnp.arange(S)
    scale = d ** -0.5
    bias = sb_bias.astype(jnp.float32)[None, :, None, None]

    def block(args):
        q_blk, start = args
        q_pos = q_offset + start + jnp.arange(qb)
        valid = k_pos[None, :] < q_pos[:, None]
        z = jnp.einsum('bqhd,bshd->bhqs', q_blk, k, preferred_element_type=jnp.float32) * scale + bias
        log_keep = jnp.where(valid, -jax.nn.softplus(z), 0.0)
        later = lax.cumsum(log_keep, axis=3, reverse=True) - log_keep
        a = jnp.where(valid, jnp.exp(jax.nn.log_sigmoid(z) + later), 0.0)
        return jnp.einsum('bhqs,bshd->bqhd', a.astype(v.dtype), v)

    q_blocks = q.reshape(B, nb, qb, H, d).transpose(1, 0, 2, 3, 4)
    starts = jnp.arange(nb) * qb
    out = lax.map(block, (q_blocks, starts))
    return out.transpose(1, 0, 2, 3, 4).reshape(B, T, H, d)


def multiscale_pool(u, buf, pos0, w_pool, pool_scale):
    B, T, _ = u.shape
    padded = jnp.concatenate([buf, u], axis=1)
    pf = padded.astype(jnp.float32)
    cs = jnp.concatenate([jnp.zeros((B, 1, D_B), jnp.float32), jnp.cumsum(pf, axis=1)], axis=1)
    hi = cs[:, POOL_BUF + 1:]
    pos = pos0 + jnp.arange(T)
    means = []
    for g, w in enumerate(POOL_WINDOWS):
        sl = slice(g * C_B, (g + 1) * C_B)
        lo = cs[:, POOL_BUF + 1 - w: POOL_BUF + 1 - w + T, sl]
        cnt = jnp.minimum(w, pos + 1).astype(jnp.float32)[None, :, None]
        means.append((hi[..., sl] - lo) / cnt)
    mean = jnp.stack(means, axis=2)
    diff = mean - u.reshape(B, T, G_B, C_B).astype(jnp.float32)
    y = jnp.einsum('btgc,gcd->btgd', diff.astype(u.dtype), w_pool).reshape(B, T, D_B)
    return y * pool_scale, padded[:, -POOL_BUF:]


def mix_even(h, k_past, v_past, pool_buf, pos0, w_in, sb_bias, w_pool, pool_scale, w_out):
    B, T, _ = h.shape
    proj = h @ w_in
    q, k, v, u = jnp.split(proj, [D_A, 2 * D_A, 3 * D_A], axis=-1)
    q = q.reshape(B, T, H_A, D_HEAD_A)
    k = k.reshape(B, T, H_A, D_HEAD_A)
    v = v.reshape(B, T, H_A, D_HEAD_A)
    k_all = jnp.concatenate([k_past, k], axis=1)
    v_all = jnp.concatenate([v_past, v], axis=1)
    o_a = stick_breaking_attention(q, k_all, v_all, sb_bias, pos0).reshape(B, T, D_A)
    o_b, new_buf = multiscale_pool(u, pool_buf, pos0, w_pool, pool_scale)
    y = jnp.concatenate([o_a, o_b], axis=-1) @ w_out
    return y, k, v, new_buf


def causal_conv(x, buf, w):
    xp = jnp.concatenate([buf, x], axis=1)
    y = lax.conv_general_dilated(xp, w[:, None, :], window_strides=(1,), padding='VALID',
                                 dimension_numbers=('NWC', 'WIO', 'NWC'),
                                 feature_group_count=x.shape[-1])
    return y, xp[:, -(CONV_W - 1):]


def gated_delta_chunked(q, k, v, g, beta, s0):
    B, T, H, _ = q.shape
    dv = v.shape[-1]
    c = CHUNK if T % CHUNK == 0 else T
    n = T // c
    f32 = jnp.float32

    def to_chunks(x):
        x = x.reshape((B, n, c, H) + x.shape[3:])
        return jnp.moveaxis(x, (1, 3), (0, 2))

    qc = to_chunks(q.astype(f32))
    kc = to_chunks(k.astype(f32))
    vc = to_chunks(v.astype(f32))
    bc = to_chunks(beta.astype(f32))
    gc = jnp.cumsum(to_chunks(g.astype(f32)), axis=-1)
    idx = jnp.arange(c)
    incl = idx[:, None] >= idx[None, :]
    strict = idx[:, None] > idx[None, :]
    diff = gc[..., :, None] - gc[..., None, :]
    decay = jnp.where(incl, jnp.exp(jnp.where(incl, diff, 0.0)), 0.0)
    k_beta = kc * bc[..., None]
    m = jnp.where(strict, jnp.einsum('nbhid,nbhjd->nbhij', k_beta, kc) * decay, 0.0)
    eye = jnp.eye(c, dtype=f32)
    rhs = jnp.concatenate([vc * bc[..., None], k_beta * jnp.exp(gc)[..., None]], axis=-1)
    sol = lax.linalg.triangular_solve(m + eye, rhs, left_side=True, lower=True, unit_diagonal=True)
    u_c, w_c = sol[..., :dv], sol[..., dv:]
    attn_intra = jnp.einsum('nbhid,nbhjd->nbhij', qc, kc) * decay
    q_dec = qc * jnp.exp(gc)[..., None]
    k_tail = kc * jnp.exp(gc[..., -1:] - gc)[..., None]
    g_last = jnp.exp(gc[..., -1])

    def step(s, xs):
        u_i, w_i, a_i, qd_i, kt_i, gl_i = xs
        v_new = u_i - jnp.einsum('bhcd,bhde->bhce', w_i, s)
        o = jnp.einsum('bhcd,bhde->bhce', qd_i, s) + jnp.einsum('bhij,bhje->bhie', a_i, v_new)
        s = s * gl_i[..., None, None] + jnp.einsum('bhcd,bhce->bhde', kt_i, v_new)
        return s, o

    s_final, o = lax.scan(step, s0.astype(f32), (u_c, w_c, attn_intra, q_dec, k_tail, g_last))
    o = jnp.moveaxis(o, (0, 2), (1, 3)).reshape(B, T, H, dv)
    return o, s_final.astype(s0.dtype)


def mix_odd(h, conv_buf, s0, w_in, conv_w, a_log, dt_bias, gn_w, w_out):
    B, T, _ = h.shape
    proj = h @ w_in
    qkv, z, a, b = jnp.split(proj, [D_CONV, D_CONV + H_C * D_V, D_CONV + H_C * D_V + H_C], axis=-1)
    qkv, new_buf = causal_conv(qkv, conv_buf, conv_w)
    qkv = jax.nn.silu(qkv)
    q, k, v = jnp.split(qkv, [H_C * D_K, 2 * H_C * D_K], axis=-1)
    q = l2norm(q.reshape(B, T, H_C, D_K)) * (D_K ** -0.5)
    k = l2norm(k.reshape(B, T, H_C, D_K))
    v = v.reshape(B, T, H_C, D_V)
    beta = jax.nn.sigmoid(b.astype(jnp.float32))
    g = -jnp.exp(a_log.astype(jnp.float32)) * jax.nn.softplus(a.astype(jnp.float32) + dt_bias.astype(jnp.float32))
    o, s_new = gated_delta_chunked(q, k, v, g, beta, s0)
    o = rmsnorm(o, gn_w) * jax.nn.silu(z.reshape(B, T, H_C, D_V).astype(jnp.float32))
    return o.reshape(B, T, H_C * D_V).astype(h.dtype) @ w_out, new_buf, s_new


def trunk(x, pos0, k_past, v_past, pool_bufs, conv_bufs, rec_states, weights):
    (norm_ffn, w_ffn_in, w_ffn_out, norm_mix, w_in_even, sb_bias, w_pool, pool_scale, w_out_even,
     w_in_odd, conv_w, a_log, dt_bias, gn_w, w_out_odd, norm_final) = weights
    new_k, new_v, new_pool, new_conv, new_rec = [], [], [], [], []
    for layer in range(DEPTH):
        i = layer // 2
        x = x + FFN_HALF * swiglu(rmsnorm(x, norm_ffn[layer, 0]), w_ffn_in[layer, 0], w_ffn_out[layer, 0])
        h = rmsnorm(x, norm_mix[layer])
        if layer % 2 == 0:
            y, k, v, pb = mix_even(h, k_past[i], v_past[i], pool_bufs[i], pos0,
                                   w_in_even[i], sb_bias[i], w_pool[i], pool_scale[i], w_out_even[i])
            new_k.append(k)
            new_v.append(v)
            new_pool.append(pb)
        else:
            y, cb, s = mix_odd(h, conv_bufs[i], rec_states[i], w_in_odd[i], conv_w[i],
                               a_log[i], dt_bias[i], gn_w[i], w_out_odd[i])
            new_conv.append(cb)
            new_rec.append(s)
        x = x + y
        x = x + FFN_HALF * swiglu(rmsnorm(x, norm_ffn[layer, 1]), w_ffn_in[layer, 1], w_ffn_out[layer, 1])
    return (rmsnorm(x, norm_final), jnp.stack(new_k), jnp.stack(new_v), jnp.stack(new_pool),
            jnp.stack(new_conv), jnp.stack(new_rec))


def setup_inputs(seed: int = 0) -> dict:
    key = jax.random.key(seed)
    ks = jax.random.split(key, 24)
    f32 = jnp.float32

    def nrm(k, shape, scale):
        return jax.random.normal(k, shape, f32) * scale

    n_pages = PAST_LEN // PAGE_SIZE
    n_used = DEC_BATCH * n_pages
    n_pool = n_used + n_used // 4
    page_table = jax.random.permutation(ks[0], n_pool)[:n_used].reshape(DEC_BATCH, n_pages).astype(jnp.int32)
    dt = jnp.exp(jax.random.uniform(ks[20], (N_ODD, H_C), f32, math.log(1e-3), math.log(1e-1)))
    dt_bias = dt + jnp.log(-jnp.expm1(-dt))
    a_log = jnp.log(jax.random.uniform(ks[21], (N_ODD, H_C), f32, 1.0, 16.0))
    return {
        'x_prompt': nrm(ks[1], (BATCH, SEQ, D_MODEL), 1.0),
        'x_sample': nrm(ks[2], (DEC_BATCH, DEC_SEQ, D_MODEL), 1.0),
        'cache_k': nrm(ks[3], (N_EVEN, n_pool, PAGE_SIZE, H_A, D_HEAD_A), 1.0),
        'cache_v': nrm(ks[4], (N_EVEN, n_pool, PAGE_SIZE, H_A, D_HEAD_A), 1.0),
        'page_table': page_table,
        'state_pool': nrm(ks[5], (N_EVEN, DEC_BATCH, POOL_BUF, D_B), 1.0),
        'state_conv': nrm(ks[6], (N_ODD, DEC_BATCH, CONV_W - 1, D_CONV), 1.0),
        'state_rec': nrm(ks[7], (N_ODD, DEC_BATCH, H_C, D_K, D_V), D_K ** -0.5),
        'norm_ffn': 1.0 + nrm(ks[8], (DEPTH, 2, D_MODEL), 0.02),
        'w_ffn_in': nrm(ks[9], (DEPTH, 2, D_MODEL, 2 * D_FF), D_MODEL ** -0.5),
        'w_ffn_out': nrm(ks[10], (DEPTH, 2, D_FF, D_MODEL), D_FF ** -0.5),
        'norm_mix': 1.0 + nrm(ks[11], (DEPTH, D_MODEL), 0.02),
        'w_in_even': nrm(ks[12], (N_EVEN, D_MODEL, D_IN_EVEN), D_MODEL ** -0.5),
        'sb_bias': SB_BIAS_INIT + nrm(ks[23], (N_EVEN, H_A), 0.3),
        'w_pool': nrm(ks[13], (N_EVEN, G_B, C_B, C_B), C_B ** -0.5),
        'pool_scale': 1.0 + nrm(ks[14], (N_EVEN, D_B), 0.1),
        'w_out_even': nrm(ks[15], (N_EVEN, D_MIX, D_MODEL), D_MIX ** -0.5),
        'w_in_odd': nrm(ks[16], (N_ODD, D_MODEL, D_IN_ODD), D_MODEL ** -0.5),
        'conv_w': nrm(ks[17], (N_ODD, CONV_W, D_CONV), CONV_W ** -0.5),
        'a_log': a_log,
        'dt_bias': dt_bias,
        'gn_w': 1.0 + nrm(ks[18], (N_ODD, D_V), 0.02),
        'w_out_odd': nrm(ks[19], (N_ODD, H_C * D_V, D_MODEL), (H_C * D_V) ** -0.5),
        'norm_final': 1.0 + nrm(ks[22], (D_MODEL,), 0.02),
    }


def reference(x_prompt, x_sample, cache_k, cache_v, page_table, state_pool, state_conv, state_rec,
              norm_ffn, w_ffn_in, w_ffn_out, norm_mix, w_in_even, sb_bias, w_pool, pool_scale, w_out_even,
              w_in_odd, conv_w, a_log, dt_bias, gn_w, w_out_odd, norm_final):
    weights = (norm_ffn, w_ffn_in, w_ffn_out, norm_mix, w_in_even, sb_bias, w_pool, pool_scale, w_out_even,
               w_in_odd, conv_w, a_log, dt_bias, gn_w, w_out_odd, norm_final)
    bp = x_prompt.shape[0]
    empty_kv = jnp.zeros((N_EVEN, bp, 0, H_A, D_HEAD_A), cache_k.dtype)
    y_prompt, k_new_p, v_new_p, pool_p, conv_p, rec_p = trunk(
        x_prompt, 0, empty_kv, empty_kv,
        jnp.zeros((N_EVEN, bp, POOL_BUF, D_B), state_pool.dtype),
        jnp.zeros((N_ODD, bp, CONV_W - 1, D_CONV), state_conv.dtype),
        jnp.zeros((N_ODD, bp, H_C, D_K, D_V), state_rec.dtype),
        weights)
    db, n_pages = page_table.shape
    past_len = n_pages * cache_k.shape[2]
    k_past = cache_k[:, page_table].reshape(N_EVEN, db, past_len, H_A, D_HEAD_A)
    v_past = cache_v[:, page_table].reshape(N_EVEN, db, past_len, H_A, D_HEAD_A)
    y_sample, k_new_s, v_new_s, pool_s, conv_s, rec_s = trunk(
        x_sample, past_len, k_past, v_past, state_pool, state_conv, state_rec, weights)
    return (y_prompt, y_sample, k_new_p, v_new_p, k_new_s, v_new_s, pool_p, pool_s, conv_p, conv_s, rec_p, rec_s)
```

```python
import functools

import jax
import jax.numpy as jnp
from jax import lax
from jax.experimental import pallas as pl
from jax.experimental.pallas import tpu as pltpu

F32 = jnp.float32
BF16 = jnp.bfloat16
EPS = 1e-6
FFN_HALF = 0.5
POOL_WINDOWS = (2, 4, 8, 16)
POOL_HALO = 16
LANES = 128
SUBLANES = 8
VMEM_LIMIT = 56 * 1024 * 1024
HIGHEST = lax.Precision.HIGHEST


def _cparams(*sem):
    return pltpu.CompilerParams(dimension_semantics=sem, vmem_limit_bytes=VMEM_LIMIT)


def _tile(n, pref):
    t = pref
    while t > SUBLANES and n % t:
        t //= 2
    assert n % t == 0, (n, pref)
    return t


def _dot(a, b):
    return jnp.dot(a, b, preferred_element_type=F32)


def _dot_nt(a, b):
    return lax.dot_general(a, b, (((1,), (1,)), ((), ())), preferred_element_type=F32)


def _dot_hi(a, b):
    return jnp.dot(a, b, preferred_element_type=F32, precision=HIGHEST)


def _rms(x, g):
    return x * lax.rsqrt(jnp.mean(x * x, axis=-1, keepdims=True) + EPS) * g


def _softplus(x):
    return jnp.maximum(x, 0.0) + jnp.log(1.0 + jnp.exp(-jnp.abs(x)))


def _silu(x):
    return x * jax.nn.sigmoid(x)


def _iota2(shape, dim):
    return lax.broadcasted_iota(jnp.int32, shape, dim)


def _resident(shape, index_map):
    return pl.BlockSpec(shape, index_map, pipeline_mode=pl.Buffered(1))


def _ffn_kernel(x_ref, g_ref, wi_ref, wo_ref, *rest, d_ff, ck, final):
    o_ref = rest[-1]
    x = x_ref[...]
    h = _rms(x, g_ref[...]).astype(BF16)
    acc = None
    for c in range(d_ff // ck):
        gate = _dot(h, wi_ref[:, c * ck:(c + 1) * ck])
        up = _dot(h, wi_ref[:, d_ff + c * ck:d_ff + (c + 1) * ck])
        act = (_silu(gate) * up).astype(BF16)
        part = _dot(act, wo_ref[c * ck:(c + 1) * ck, :])
        acc = part if acc is None else acc + part
    y = x + FFN_HALF * acc
    if final:
        y = _rms(y, rest[0][...])
    o_ref[...] = y


def _ffn(x, norm_ffn, w_in, w_out, layer, j, final_g=None):
    n, d = x.shape
    d_ff = w_out.shape[2]
    tm = _tile(n, 512)
    ck = 256 if d_ff % 256 == 0 else d_ff
    in_specs = [
        pl.BlockSpec((tm, d), lambda i: (i, 0)),
        pl.BlockSpec((None, None, 1, d), lambda i: (layer, j, 0, 0)),
        _resident((None, None, d, 2 * d_ff), lambda i: (layer, j, 0, 0)),
        _resident((None, None, d_ff, d), lambda i: (layer, j, 0, 0)),
    ]
    args = [x, norm_ffn, w_in, w_out]
    if final_g is not None:
        in_specs.append(pl.BlockSpec((1, d), lambda i: (0, 0)))
        args.append(final_g)
    return pl.pallas_call(
        functools.partial(_ffn_kernel, d_ff=d_ff, ck=ck, final=final_g is not None),
        grid=(n // tm,),
        in_specs=in_specs,
        out_specs=pl.BlockSpec((tm, d), lambda i: (i, 0)),
        out_shape=jax.ShapeDtypeStruct((n, d), F32),
        compiler_params=_cparams("parallel"),
        name="ffn",
    )(*args)


def _proj_kernel(x_ref, g_ref, w_ref, *o_refs, segs):
    h = _rms(x_ref[...], g_ref[...]).astype(BF16)
    for (a, b, outs) in segs:
        y = _dot(h, w_ref[:, a:b])
        for oi in outs:
            o_refs[oi][...] = y.astype(o_refs[oi].dtype)


def _proj(x, g, w, segs, out_dtypes, out_widths, name):
    n, d = x.shape
    tm = _tile(n, 512)
    g_arr, g_idx = g
    w_arr, w_idx = w
    in_specs = [
        pl.BlockSpec((tm, d), lambda i: (i, 0)),
        pl.BlockSpec((None, 1, d), lambda i: (g_idx, 0, 0)),
        _resident((None, d, w_arr.shape[2]), lambda i: (w_idx, 0, 0)),
    ]
    return pl.pallas_call(
        functools.partial(_proj_kernel, segs=segs),
        grid=(n // tm,),
        in_specs=in_specs,
        out_specs=[pl.BlockSpec((tm, wd), lambda i: (i, 0)) for wd in out_widths],
        out_shape=[jax.ShapeDtypeStruct((n, wd), dt) for wd, dt in zip(out_widths, out_dtypes)],
        compiler_params=_cparams("parallel"),
        name=name,
    )(x, g_arr, w_arr)


def _sb_tile(z, r_run, valid, u_tri):
    sp = _softplus(z)
    lk = -sp
    if valid is not None:
        lk = jnp.where(valid, lk, 0.0)
    hi = lk.astype(BF16)
    lo = (lk - hi.astype(F32)).astype(BF16)
    later = _dot(hi, u_tri) + _dot(lo, u_tri)
    a = jnp.exp(z - sp + later + r_run)
    if valid is not None:
        a = jnp.where(valid, a, 0.0)
    return a, jnp.sum(lk, axis=1, keepdims=True)


def _attn_prompt_kernel(bias_ref, q_ref, k_ref, v_ref, o_ref, *, tq, dh, scale):
    hp = pl.program_id(1)
    i = pl.program_id(2)
    q = q_ref[...] * scale
    lane = _iota2((tq, LANES), 1)
    row = _iota2((tq, tq), 0)
    col = _iota2((tq, tq), 1)
    u_tri = jnp.where(row > col, 1.0, 0.0).astype(BF16)
    causal = col < row
    out = jnp.zeros((tq, LANES), F32)
    for e in range(LANES // dh):
        head = (lane >= e * dh) & (lane < (e + 1) * dh)
        qe = jnp.where(head, q, 0.0).astype(BF16)
        bias = bias_ref[0, hp * (LANES // dh) + e]

        def tile(j, carry, valid, qe=qe, bias=bias):
            r_run, acc = carry
            r0 = pl.multiple_of(j * tq, tq)
            kj = k_ref[pl.ds(r0, tq), :]
            vj = v_ref[pl.ds(r0, tq), :]
            z = _dot_nt(qe, kj) + bias
            a, rs = _sb_tile(z, r_run, valid, u_tri)
            return r_run + rs, acc + _dot(a.astype(BF16), vj)

        carry = tile(i, (jnp.zeros((tq, 1), F32), jnp.zeros((tq, LANES), F32)), causal)
        carry = lax.fori_loop(0, i, lambda s, c: tile(i - 1 - s, c, None), carry)
        out = out + jnp.where(head, carry[1], 0.0)
    o_ref[...] = out


def _attn_prompt(q, k_bf, v_bf, sb_bias, batch, seq, dh):
    n, da = q.shape
    tq = _tile(seq, 256)
    nq = seq // tq
    return pl.pallas_call(
        functools.partial(_attn_prompt_kernel, tq=tq, dh=dh, scale=dh ** -0.5),
        grid=(batch, da // LANES, nq),
        in_specs=[
            pl.BlockSpec(memory_space=pltpu.SMEM),
            pl.BlockSpec((tq, LANES), lambda b, hp, i: (b * nq + i, hp)),
            pl.BlockSpec((seq, LANES), lambda b, hp, i: (b, hp)),
            pl.BlockSpec((seq, LANES), lambda b, hp, i: (b, hp)),
        ],
        out_specs=pl.BlockSpec((tq, LANES), lambda b, hp, i: (b * nq + i, hp)),
        out_shape=jax.ShapeDtypeStruct((n, da), F32),
        compiler_params=_cparams("parallel", "parallel", "arbitrary"),
        name="attn_prompt",
    )(sb_bias, q, k_bf, v_bf)


def _attn_sample_kernel(pt_ref, bias_ref, q_ref, kn_ref, vn_ref, *rest, n_pages, page, t_new, heads, dh,
                        scale):
    del pt_ref
    k_pages = rest[:n_pages]
    v_pages = rest[n_pages:2 * n_pages]
    o_ref = rest[2 * n_pages]
    da = heads * dh
    ht = heads * t_new
    q = q_ref[...] * scale
    qt = jnp.concatenate([q] * heads, axis=0)
    r_h = _iota2((ht, da), 0) // t_new
    c_h = _iota2((ht, da), 1) // dh
    blockdiag = r_h == c_h
    qbd = jnp.where(blockdiag, qt, 0.0).astype(BF16)
    rh1 = _iota2((ht, 1), 0) // t_new
    bias = jnp.zeros((ht, 1), F32)
    for h in range(heads):
        bias = jnp.where(rh1 == h, bias_ref[0, h], bias)
    row = _iota2((page, page), 0)
    col = _iota2((page, page), 1)
    u_tri = jnp.where(row > col, 1.0, 0.0).astype(BF16)

    def tile(kf, vf, carry, valid):
        r_run, acc = carry
        z = _dot_nt(qbd, kf.astype(BF16)) + bias
        a, rs = _sb_tile(z, r_run, valid, u_tri)
        return r_run + rs, acc + _dot(a.astype(BF16), vf.astype(BF16))

    pad = jnp.zeros((page - t_new, da), F32)
    kn = jnp.concatenate([kn_ref[...], pad], axis=0)
    vn = jnp.concatenate([vn_ref[...], pad], axis=0)
    t_of_row = _iota2((ht, page), 0) % t_new
    valid_new = _iota2((ht, page), 1) < t_of_row
    carry = tile(kn, vn, (jnp.zeros((ht, 1), F32), jnp.zeros((ht, da), F32)), valid_new)
    for p in reversed(range(n_pages)):
        carry = tile(k_pages[p][...], v_pages[p][...], carry, None)
    acc = jnp.where(blockdiag, carry[1], 0.0)
    out = acc[0:t_new]
    for h in range(1, heads):
        out = out + acc[h * t_new:(h + 1) * t_new]
    o_ref[...] = out


def _attn_sample(q, k_new, v_new, cache_k, cache_v, layer, page_table, sb_bias, heads, dh, t_new):
    n, da = q.shape
    db, n_pages = page_table.shape
    page = cache_k.shape[2]
    page_specs = [
        pl.BlockSpec((None, None, page, da), lambda b, pt, p=p: (layer, pt[b, p], 0, 0))
        for p in range(n_pages)
    ]
    row_spec = pl.BlockSpec((t_new, da), lambda b, pt: (b, 0))
    grid_spec = pltpu.PrefetchScalarGridSpec(
        num_scalar_prefetch=1,
        grid=(db,),
        in_specs=[pl.BlockSpec(memory_space=pltpu.SMEM), row_spec, row_spec, row_spec]
        + page_specs + page_specs,
        out_specs=row_spec,
    )
    return pl.pallas_call(
        functools.partial(_attn_sample_kernel, n_pages=n_pages, page=page, t_new=t_new, heads=heads,
                          dh=dh, scale=dh ** -0.5),
        grid_spec=grid_spec,
        out_shape=jax.ShapeDtypeStruct((n, da), F32),
        compiler_params=_cparams("parallel"),
        name="attn_sample",
    )(page_table, sb_bias, q, k_new, v_new, *([cache_k] * n_pages), *([cache_v] * n_pages))


def _even_out_tail(x, oa, diffs, wp_ref, ps_ref, wo_ref):
    da = oa.shape[1]
    y = _dot(oa.astype(BF16), wo_ref[0:da, :])
    for g, diff in enumerate(diffs):
        cb = diff.shape[1]
        ob = _dot(diff.astype(BF16), wp_ref[g]) * ps_ref[:, g * cb:(g + 1) * cb]
        y = y + _dot(ob.astype(BF16), wo_ref[da + g * cb:da + (g + 1) * cb, :])
    return x + y


def _even_out_prompt_kernel(x_ref, oa_ref, u_ref, halo_ref, wp_ref, ps_ref, wo_ref, o_ref, pad_scr, *,
                            tp, tiles_per_seq, cb):
    it = pl.program_id(0) % tiles_per_seq
    pad_scr[0:POOL_HALO, :] = jnp.where(it == 0, 0.0, halo_ref[...])
    pad_scr[POOL_HALO:POOL_HALO + tp, :] = u_ref[...]
    pos = it * tp + _iota2((tp, 1), 0)
    diffs = []
    for g, w in enumerate(POOL_WINDOWS):
        cols = slice(g * cb, (g + 1) * cb)
        s = pad_scr[POOL_HALO:POOL_HALO + tp, cols]
        for j in range(1, w):
            s = s + pad_scr[POOL_HALO - j:POOL_HALO - j + tp, cols]
        cnt = jnp.minimum(w, pos + 1).astype(F32)
        diffs.append(s / cnt - u_ref[:, cols])
    o_ref[...] = _even_out_tail(x_ref[...], oa_ref[...], diffs, wp_ref, ps_ref, wo_ref)


def _even_out_prompt(x, oa, u, w_pool, pool_scale, w_out, layer, seq):
    n, d = x.shape
    db_ = u.shape[1]
    tp = _tile(seq, 512)
    tiles_per_seq = seq // tp
    hb = tp // POOL_HALO
    g_b = w_pool.shape[1]
    cb = db_ // g_b
    return pl.pallas_call(
        functools.partial(_even_out_prompt_kernel, tp=tp, tiles_per_seq=tiles_per_seq, cb=cb),
        grid=(n // tp,),
        in_specs=[
            pl.BlockSpec((tp, d), lambda i: (i, 0)),
            pl.BlockSpec((tp, oa.shape[1]), lambda i: (i, 0)),
            pl.BlockSpec((tp, db_), lambda i: (i, 0)),
            pl.BlockSpec((POOL_HALO, db_), lambda i: (jnp.maximum(i * hb - 1, 0), 0)),
            _resident((None, g_b, cb, cb), lambda i: (layer, 0, 0, 0)),
            pl.BlockSpec((None, 1, db_), lambda i: (layer, 0, 0)),
            _resident((None, d, d), lambda i: (layer, 0, 0)),
        ],
        out_specs=pl.BlockSpec((tp, d), lambda i: (i, 0)),
        out_shape=jax.ShapeDtypeStruct((n, d), F32),
        scratch_shapes=[pltpu.VMEM((POOL_HALO + tp, db_), F32)],
        compiler_params=_cparams("parallel"),
        name="even_out_prompt",
    )(x, oa, u, u, w_pool, pool_scale, w_out)


def _even_out_sample_kernel(x_ref, oa_ref, p_ref, wp_ref, ps_ref, wo_ref, o_ref, *, bb, t_new, pos0, cb):
    pos = pos0 + _iota2((1, t_new, 1), 1)
    diffs = []
    for g, w in enumerate(POOL_WINDOWS):
        cols = slice(g * cb, (g + 1) * cb)
        u3 = p_ref[:, POOL_HALO:POOL_HALO + t_new, cols]
        s = u3
        for j in range(1, w):
            s = s + p_ref[:, POOL_HALO - j:POOL_HALO - j + t_new, cols]
        cnt = jnp.minimum(w, pos + 1).astype(F32)
        diffs.append((s / cnt - u3).reshape(bb * t_new, cb))
    o_ref[...] = _even_out_tail(x_ref[...], oa_ref[...], diffs, wp_ref, ps_ref, wo_ref)


def _even_out_sample(x, oa, padded, w_pool, pool_scale, w_out, layer, t_new, pos0):
    n, d = x.shape
    nb, rows, db_ = padded.shape
    bb = max(b for b in range(1, min(nb, 64) + 1) if nb % b == 0)
    g_b = w_pool.shape[1]
    cb = db_ // g_b
    return pl.pallas_call(
        functools.partial(_even_out_sample_kernel, bb=bb, t_new=t_new, pos0=pos0, cb=cb),
        grid=(nb // bb,),
        in_specs=[
            pl.BlockSpec((bb * t_new, d), lambda i: (i, 0)),
            pl.BlockSpec((bb * t_new, oa.shape[1]), lambda i: (i, 0)),
            pl.BlockSpec((bb, rows, db_), lambda i: (i, 0, 0)),
            _resident((None, g_b, cb, cb), lambda i: (layer, 0, 0, 0)),
            pl.BlockSpec((None, 1, db_), lambda i: (layer, 0, 0)),
            _resident((None, d, d), lambda i: (layer, 0, 0)),
        ],
        out_specs=pl.BlockSpec((bb * t_new, d), lambda i: (i, 0)),
        out_shape=jax.ShapeDtypeStruct((n, d), F32),
        compiler_params=_cparams("parallel"),
        name="even_out_sample",
    )(x, oa, padded, w_pool, pool_scale, w_out)


CONV_PAD = SUBLANES


def _unit_lower_inverse(m, eye):
    n = m.shape[0]
    p = eye - m
    x = m
    k = 2
    while k < n:
        x = _dot_hi(x, x)
        p = p + _dot_hi(p, x)
        k *= 2
    return p


def _gdn_kernel(alog_ref, dtb_ref, qkv_ref, z_ref, ab_ref, cbuf_ref, s0_ref, cw_ref, gn_ref,
                o_ref, sfin_ref, xpad, q_scr, k_scr, v_scr, g_scr, b_scr, s_scr, *,
                tb, chunk, heads, dk, dv, cw):
    t = pl.program_id(1)
    tail = cw - 1

    @pl.when(t == 0)
    def _():
        s_scr[...] = s0_ref[...]
        xpad[CONV_PAD - tail:CONV_PAD, :] = cbuf_ref[...]

    xpad[CONV_PAD:CONV_PAD + tb, :] = qkv_ref[...]

    def conv_silu(c0, width):
        acc = None
        for i in range(cw):
            r0 = CONV_PAD - tail + i
            term = xpad[r0:r0 + tb, c0:c0 + width] * cw_ref[i:i + 1, c0:c0 + width]
            acc = term if acc is None else acc + term
        return _silu(acc)

    def l2n(x):
        return x * lax.rsqrt(jnp.sum(x * x, axis=-1, keepdims=True) + EPS)

    for h in range(heads):
        q_scr[:, h * dk:(h + 1) * dk] = l2n(conv_silu(h * dk, dk)) * (dk ** -0.5)
        k_scr[:, h * dk:(h + 1) * dk] = l2n(conv_silu(heads * dk + h * dk, dk))
        v_scr[:, h * dv:(h + 1) * dv] = conv_silu(2 * heads * dk + h * dv, dv)
    ab = ab_ref[...]
    g_scr[...] = -jnp.exp(alog_ref[...]) * _softplus(ab + dtb_ref[...])
    b_scr[...] = jax.nn.sigmoid(ab)

    xpad[CONV_PAD - tail:CONV_PAD, :] = xpad[CONV_PAD + tb - tail:CONV_PAD + tb, :]

    row = _iota2((chunk, chunk), 0)
    col = _iota2((chunk, chunk), 1)
    lower = row >= col
    strict = row > col
    l_incl = jnp.where(lower, 1.0, 0.0)
    eye = jnp.where(row == col, 1.0, 0.0)

    def do_chunk(c, _):
        r0 = pl.multiple_of(c * chunk, chunk)
        rows = pl.ds(r0, chunk)
        g_all = g_scr[rows, :]
        b_all = b_scr[rows, :]
        for h in range(heads):
            q = q_scr[rows, h * dk:(h + 1) * dk]
            k = k_scr[rows, h * dk:(h + 1) * dk]
            v = v_scr[rows, h * dv:(h + 1) * dv]
            g_col = g_all[:, h:h + 1]
            beta = b_all[:, heads + h:heads + h + 1]
            kb = k * beta
            gc = _dot_hi(l_incl, jnp.broadcast_to(g_col, (chunk, dk)))
            diff = _dot_hi(l_incl, jnp.where(strict, g_col, 0.0))
            decay = jnp.where(lower, jnp.exp(jnp.where(lower, diff, 0.0)), 0.0)
            kq = _dot_nt(jnp.concatenate([kb, q], axis=0).astype(BF16), k.astype(BF16))
            m = jnp.where(strict, kq[0:chunk] * decay, 0.0)
            attn = kq[chunk:2 * chunk] * decay
            t_inv = _unit_lower_inverse(m, eye)
            eg = jnp.exp(gc)
            sol = _dot_hi(t_inv, jnp.concatenate([v * beta, kb * eg], axis=1))
            u_c = sol[:, 0:dv]
            w_c = sol[:, dv:dv + dk]
            gc_last = gc[chunk - 1:chunk, :]
            k_tail = k * jnp.exp(gc_last - gc)
            s = s_scr[h]
            ws = _dot(jnp.concatenate([w_c, q * eg], axis=0).astype(BF16), s.astype(BF16))
            v_new = u_c - ws[0:chunk]
            o = ws[chunk:2 * chunk] + _dot(attn.astype(BF16), v_new.astype(BF16))
            s_scr[h] = s * jnp.exp(gc_last) + lax.dot_general(
                k_tail.astype(BF16), v_new.astype(BF16), (((0,), (0,)), ((), ())),
                preferred_element_type=F32)
            zg = z_ref[rows, h * dv:(h + 1) * dv]
            o_ref[rows, h * dv:(h + 1) * dv] = _rms(o, gn_ref[...]) * _silu(zg)
        return 0

    lax.fori_loop(0, tb // chunk, do_chunk, 0)

    @pl.when(t == pl.num_programs(1) - 1)
    def _():
        sfin_ref[...] = s_scr[...]


def _gdn(qkv, z, ab, conv_buf, s0, conv_w, a_log_row, dt_bias_row, gn_w, layer, batch, seq, chunk_pref):
    n, dconv = qkv.shape
    _, heads, dk, dv = s0.shape
    cw = conv_w.shape[1]
    tb = _tile(seq, 256)
    chunk = _tile(tb, chunk_pref)
    nt = seq // tb
    dz = heads * dv
    row_map = lambda b, t: (b * nt + t, 0)
    return pl.pallas_call(
        functools.partial(_gdn_kernel, tb=tb, chunk=chunk, heads=heads, dk=dk, dv=dv, cw=cw),
        grid=(batch, nt),
        in_specs=[
            pl.BlockSpec((None, 1, LANES), lambda b, t: (layer, 0, 0)),
            pl.BlockSpec((None, 1, LANES), lambda b, t: (layer, 0, 0)),
            pl.BlockSpec((tb, dconv), row_map),
            pl.BlockSpec((tb, dz), row_map),
            pl.BlockSpec((tb, LANES), row_map),
            pl.BlockSpec((None, cw - 1, dconv), lambda b, t: (b, 0, 0)),
            pl.BlockSpec((None, heads, dk, dv), lambda b, t: (b, 0, 0, 0)),
            pl.BlockSpec((None, cw, dconv), lambda b, t: (layer, 0, 0)),
            pl.BlockSpec((None, 1, dv), lambda b, t: (layer, 0, 0)),
        ],
        out_specs=[
            pl.BlockSpec((tb, dz), row_map),
            pl.BlockSpec((None, heads, dk, dv), lambda b, t: (b, 0, 0, 0)),
        ],
        out_shape=[
            jax.ShapeDtypeStruct((n, dz), F32),
            jax.ShapeDtypeStruct((batch, heads, dk, dv), F32),
        ],
        scratch_shapes=[
            pltpu.VMEM((CONV_PAD + tb, dconv), F32),
            pltpu.VMEM((tb, heads * dk), F32),
            pltpu.VMEM((tb, heads * dk), F32),
            pltpu.VMEM((tb, heads * dv), F32),
            pltpu.VMEM((tb, LANES), F32),
            pltpu.VMEM((tb, LANES), F32),
            pltpu.VMEM((heads, dk, dv), F32),
        ],
        compiler_params=_cparams("parallel", "arbitrary"),
        name="gdn",
    )(a_log_row, dt_bias_row, qkv, z, ab, conv_buf, s0, conv_w, gn_w)


def _out_proj_kernel(x_ref, y_ref, w_ref, o_ref):
    o_ref[...] = x_ref[...] + _dot(y_ref[...].astype(BF16), w_ref[...])


def _out_proj(x, y, w, layer):
    n, d = x.shape
    tm = _tile(n, 512)
    return pl.pallas_call(
        _out_proj_kernel,
        grid=(n // tm,),
        in_specs=[
            pl.BlockSpec((tm, d), lambda i: (i, 0)),
            pl.BlockSpec((tm, y.shape[1]), lambda i: (i, 0)),
            _resident((None, y.shape[1], d), lambda i: (layer, 0, 0)),
        ],
        out_specs=pl.BlockSpec((tm, d), lambda i: (i, 0)),
        out_shape=jax.ShapeDtypeStruct((n, d), F32),
        compiler_params=_cparams("parallel"),
        name="out_proj",
    )(x, y, w)


def _trunk(x, batch, seq, pos0, cache, pool_bufs, conv_bufs, rec_states, w):
    depth = w["norm_mix"].shape[0]
    heads_a, dh = w["heads_a"], w["dh"]
    da = heads_a * dh
    new_k, new_v, new_pool, new_conv, new_rec = [], [], [], [], []
    for layer in range(depth):
        i = layer // 2
        x = _ffn(x, w["norm_ffn"], w["w_ffn_in"], w["w_ffn_out"], layer, 0)
        if layer % 2 == 0:
            db_ = w["w_in_even"].shape[2] - 3 * da
            segs = [(0, da, (0,)), (da, 2 * da, (1,)), (2 * da, 3 * da, (2,)), (3 * da, 3 * da + db_, (3,))]
            dts = [F32, F32, F32, F32]
            wds = [da, da, da, db_]
            if cache is None:
                segs[1] = (da, 2 * da, (1, 4))
                segs[2] = (2 * da, 3 * da, (2, 5))
                dts += [BF16, BF16]
                wds += [da, da]
            outs = _proj(x, (w["norm_mix"], layer), (w["w_in_even"], i), segs, dts, wds, "in_even")
            q, k, v, u = outs[:4]
            if cache is None:
                oa = _attn_prompt(q, outs[4], outs[5], w["sb_bias"][i][None], batch, seq, dh)
                x = _even_out_prompt(x, oa, u, w["w_pool"], w["pool_scale"], w["w_out_even"], i, seq)
                new_pool.append(u.reshape(batch, seq, db_)[:, seq - (POOL_HALO - 1):])
            else:
                cache_k, cache_v, page_table = cache
                oa = _attn_sample(q, k, v, cache_k, cache_v, i, page_table, w["sb_bias"][i][None],
                                  heads_a, dh, seq)
                u3 = u.reshape(batch, seq, db_)
                buf = pool_bufs[i]
                padded = jnp.concatenate(
                    [jnp.zeros((batch, POOL_HALO - buf.shape[1], db_), F32), buf, u3], axis=1)
                x = _even_out_sample(x, oa, padded, w["w_pool"], w["pool_scale"], w["w_out_even"], i,
                                     seq, pos0)
                new_pool.append(padded[:, padded.shape[1] - buf.shape[1]:])
            new_k.append(k.reshape(batch, seq, heads_a, dh))
            new_v.append(v.reshape(batch, seq, heads_a, dh))
        else:
            heads_c, dk, dv = rec_states.shape[2:]
            dconv = heads_c * (2 * dk + dv)
            dz = heads_c * dv
            segs = [(0, dconv, (0,)), (dconv, dconv + dz, (1,)), (dconv + dz, dconv + dz + LANES, (2,))]
            qkv, z, ab = _proj(x, (w["norm_mix"], layer), (w["w_in_odd"], i), segs, [F32] * 3,
                               [dconv, dz, LANES], "in_odd")
            og, s_new = _gdn(qkv, z, ab, conv_bufs[i], rec_states[i], w["conv_w"], w["a_log"],
                             w["dt_bias"], w["gn_w"], i, batch, seq, 64)
            x = _out_proj(x, og, w["w_out_odd"], i)
            tail = conv_bufs.shape[2]
            xp = jnp.concatenate([conv_bufs[i], qkv.reshape(batch, seq, dconv)], axis=1)
            new_conv.append(xp[:, xp.shape[1] - tail:])
            new_rec.append(s_new)
        final_g = w["norm_final"] if layer == depth - 1 else None
        x = _ffn(x, w["norm_ffn"], w["w_ffn_in"], w["w_ffn_out"], layer, 1, final_g)
    return x, jnp.stack(new_k), jnp.stack(new_v), jnp.stack(new_pool), jnp.stack(new_conv), jnp.stack(new_rec)


def kernel(x_prompt, x_sample, cache_k, cache_v, page_table, state_pool, state_conv, state_rec, norm_ffn,
           w_ffn_in, w_ffn_out, norm_mix, w_in_even, sb_bias, w_pool, pool_scale, w_out_even, w_in_odd,
           conv_w, a_log, dt_bias, gn_w, w_out_odd, norm_final):
    bp, seq, d = x_prompt.shape
    db, dseq, _ = x_sample.shape
    n_even, n_pool, page, heads_a, dh = cache_k.shape
    n_odd, _, heads_c, dk, dv = state_rec.shape
    da = heads_a * dh
    dconv = heads_c * (2 * dk + dv)
    dz = heads_c * dv
    depth = norm_mix.shape[0]

    def lane_row(p):
        return jnp.pad(p.astype(F32), ((0, 0), (0, LANES - p.shape[1])))[:, None, :]

    w_odd = jnp.concatenate(
        [w_in_odd, jnp.zeros((n_odd, d, LANES - 2 * heads_c), w_in_odd.dtype)], axis=2).astype(BF16)
    w = dict(
        heads_a=heads_a, dh=dh,
        norm_ffn=norm_ffn.reshape(depth, 2, 1, d),
        w_ffn_in=w_ffn_in.astype(BF16),
        w_ffn_out=w_ffn_out.astype(BF16),
        norm_mix=norm_mix.reshape(depth, 1, d),
        w_in_even=w_in_even.astype(BF16),
        sb_bias=sb_bias.astype(F32),
        w_pool=w_pool.astype(BF16),
        pool_scale=pool_scale.reshape(n_even, 1, -1),
        w_out_even=w_out_even.astype(BF16),
        w_in_odd=w_odd,
        conv_w=conv_w,
        a_log=lane_row(a_log),
        dt_bias=lane_row(dt_bias),
        gn_w=gn_w.reshape(n_odd, 1, dv),
        w_out_odd=w_out_odd.astype(BF16),
        norm_final=norm_final.reshape(1, d),
    )
    ck = cache_k.reshape(n_even, n_pool, page, da)
    cv = cache_v.reshape(n_even, n_pool, page, da)
    past_len = page_table.shape[1] * page

    yp, kp, vp, poolp, convp, recp = _trunk(
        x_prompt.reshape(bp * seq, d), bp, seq, 0, None,
        None,
        jnp.zeros((n_odd, bp, conv_w.shape[1] - 1, dconv), state_conv.dtype),
        jnp.zeros((n_odd, bp, heads_c, dk, dv), state_rec.dtype), w)
    ys, ks, vs, pools, convs, recs = _trunk(
        x_sample.reshape(db * dseq, d), db, dseq, past_len, (ck, cv, page_table),
        state_pool, state_conv, state_rec, w)
    return (yp.reshape(bp, seq, d), ys.reshape(db, dseq, d), kp, vp, ks, vs, poolp, pools, convp, convs,
            recp, recs)
```

```python
import functools

import jax
import jax.numpy as jnp
from jax import lax
from jax.experimental import pallas as pl
from jax.experimental.pallas import tpu as pltpu

F32 = jnp.float32
BF16 = jnp.bfloat16
EPS = 1e-6
FFN_HALF = 0.5
POOL_WINDOWS = (2, 4, 8, 16)
POOL_HALO = 16
LANES = 128
SUBLANES = 8
VMEM_LIMIT = 56 * 1024 * 1024
HIGHEST = lax.Precision.HIGHEST


def _cparams(*sem):
    return pltpu.CompilerParams(dimension_semantics=sem, vmem_limit_bytes=VMEM_LIMIT)


def _tile(n, pref):
    t = pref
    while t > SUBLANES and n % t:
        t //= 2
    assert n % t == 0, (n, pref)
    return t


def _dot(a, b):
    return jnp.dot(a, b, preferred_element_type=F32)


def _dot_nt(a, b):
    return lax.dot_general(a, b, (((1,), (1,)), ((), ())), preferred_element_type=F32)


def _dot_hi(a, b):
    return jnp.dot(a, b, preferred_element_type=F32, precision=HIGHEST)


def _rms(x, g):
    return x * lax.rsqrt(jnp.mean(x * x, axis=-1, keepdims=True) + EPS) * g


def _softplus(x):
    return jnp.maximum(x, 0.0) + jnp.log(1.0 + jnp.exp(-jnp.abs(x)))


def _silu(x):
    return x * jax.nn.sigmoid(x)


def _iota2(shape, dim):
    return lax.broadcasted_iota(jnp.int32, shape, dim)


def _resident(shape, index_map):
    return pl.BlockSpec(shape, index_map, pipeline_mode=pl.Buffered(1))


def _ffn_kernel(x_ref, g_ref, wi_ref, wo_ref, *rest, d_ff, ck, final):
    o_ref = rest[-1]
    x = x_ref[...]
    h = _rms(x, g_ref[...]).astype(BF16)
    acc = None
    for c in range(d_ff // ck):
        gate = _dot(h, wi_ref[:, c * ck:(c + 1) * ck])
        up = _dot(h, wi_ref[:, d_ff + c * ck:d_ff + (c + 1) * ck])
        act = (_silu(gate) * up).astype(BF16)
        part = _dot(act, wo_ref[c * ck:(c + 1) * ck, :])
        acc = part if acc is None else acc + part
    y = x + FFN_HALF * acc
    if final:
        y = _rms(y, rest[0][...])
    o_ref[...] = y


def _ffn(x, norm_ffn, w_in, w_out, layer, j, final_g=None):
    n, d = x.shape
    d_ff = w_out.shape[2]
    tm = _tile(n, 512)
    ck = 256 if d_ff % 256 == 0 else d_ff
    in_specs = [
        pl.BlockSpec((tm, d), lambda i: (i, 0)),
        pl.BlockSpec((None, None, 1, d), lambda i: (layer, j, 0, 0)),
        _resident((None, None, d, 2 * d_ff), lambda i: (layer, j, 0, 0)),
        _resident((None, None, d_ff, d), lambda i: (layer, j, 0, 0)),
    ]
    args = [x, norm_ffn, w_in, w_out]
    if final_g is not None:
        in_specs.append(pl.BlockSpec((1, d), lambda i: (0, 0)))
        args.append(final_g)
    return pl.pallas_call(
        functools.partial(_ffn_kernel, d_ff=d_ff, ck=ck, final=final_g is not None),
        grid=(n // tm,),
        in_specs=in_specs,
        out_specs=pl.BlockSpec((tm, d), lambda i: (i, 0)),
        out_shape=jax.ShapeDtypeStruct((n, d), F32),
        compiler_params=_cparams("parallel"),
        name="ffn",
    )(*args)


def _proj_kernel(x_ref, g_ref, w_ref, *o_refs, segs):
    h = _rms(x_ref[...], g_ref[...]).astype(BF16)
    for (a, b, outs) in segs:
        y = _dot(h, w_ref[:, a:b])
        for oi in outs:
            o_refs[oi][...] = y.astype(o_refs[oi].dtype)


def _proj(x, g, w, segs, out_dtypes, out_widths, name):
    n, d = x.shape
    tm = _tile(n, 512)
    g_arr, g_idx = g
    w_arr, w_idx = w
    in_specs = [
        pl.BlockSpec((tm, d), lambda i: (i, 0)),
        pl.BlockSpec((None, 1, d), lambda i: (g_idx, 0, 0)),
        _resident((None, d, w_arr.shape[2]), lambda i: (w_idx, 0, 0)),
    ]
    return pl.pallas_call(
        functools.partial(_proj_kernel, segs=segs),
        grid=(n // tm,),
        in_specs=in_specs,
        out_specs=[pl.BlockSpec((tm, wd), lambda i: (i, 0)) for wd in out_widths],
        out_shape=[jax.ShapeDtypeStruct((n, wd), dt) for wd, dt in zip(out_widths, out_dtypes)],
        compiler_params=_cparams("parallel"),
        name=name,
    )(x, g_arr, w_arr)


def _suffix_matrix(n):
    row = _iota2((2 * n, n), 0) % n
    col = _iota2((2 * n, n), 1)
    return jnp.where(row > col, 1.0, 0.0).astype(BF16)


def _sb_tile(z, r_run, valid, u_tri):
    sp = _softplus(z)
    lk = -sp
    if valid is not None:
        lk = jnp.where(valid, lk, 0.0)
    hi = lk.astype(BF16)
    lo = (lk - hi.astype(F32)).astype(BF16)
    later = _dot(jnp.concatenate([hi, lo], axis=1), u_tri)
    a = jnp.exp(z - sp + later + r_run)
    if valid is not None:
        a = jnp.where(valid, a, 0.0)
    return a, jnp.sum(lk, axis=1, keepdims=True)


ATTN_LANE_BLOCKS = 2


def _attn_prompt_kernel(bias_ref, q_ref, k_ref, v_ref, o_ref, r_scr, acc_scr, *, tq, dh, scale, groups):
    hg = pl.program_id(1)
    i = pl.program_id(2)
    nh = LANES // dh
    lane = _iota2((tq, LANES), 1)
    heads = [(lane >= e * dh) & (lane < (e + 1) * dh) for e in range(nh)]
    u_tri = _suffix_matrix(tq)
    causal = _iota2((nh * tq, tq), 1) < (_iota2((nh * tq, tq), 0) % tq)
    qs, biases = [], []
    for g in range(groups):
        q = q_ref[:, g * LANES:(g + 1) * LANES] * scale
        qs.append(jnp.concatenate([jnp.where(hm, q, 0.0) for hm in heads], axis=0).astype(BF16))
        biases.append([bias_ref[0, (hg * groups + g) * nh + e] for e in range(nh)])

    def tile(j, valid):
        r0 = pl.multiple_of(j * tq, tq)
        for g in range(groups):
            kj = k_ref[pl.ds(r0, tq), g * LANES:(g + 1) * LANES]
            vj = v_ref[pl.ds(r0, tq), g * LANES:(g + 1) * LANES]
            z = _dot_nt(qs[g], kj)
            z = jnp.concatenate([z[e * tq:(e + 1) * tq] + biases[g][e] for e in range(nh)], axis=0)
            a, rs = _sb_tile(z, r_scr[g], valid, u_tri)
            acc_scr[g] += _dot(a.astype(BF16), vj)
            r_scr[g] += rs

    r_scr[...] = jnp.zeros_like(r_scr)
    acc_scr[...] = jnp.zeros_like(acc_scr)
    tile(i, causal)

    def body(s, c):
        tile(i - 1 - s, None)
        return c

    lax.fori_loop(0, i, body, 0)
    for g in range(groups):
        out = jnp.where(heads[0], acc_scr[g, 0:tq], 0.0)
        for e in range(1, nh):
            out = out + jnp.where(heads[e], acc_scr[g, e * tq:(e + 1) * tq], 0.0)
        o_ref[:, g * LANES:(g + 1) * LANES] = out


def _attn_prompt(q, k_bf, v_bf, sb_bias, batch, seq, dh):
    n, da = q.shape
    tq = _tile(seq, 256)
    nq = seq // tq
    groups = ATTN_LANE_BLOCKS if da % (ATTN_LANE_BLOCKS * LANES) == 0 else 1
    wl = groups * LANES
    rows = LANES // dh * tq
    return pl.pallas_call(
        functools.partial(_attn_prompt_kernel, tq=tq, dh=dh, scale=dh ** -0.5, groups=groups),
        grid=(batch, da // wl, nq),
        in_specs=[
            pl.BlockSpec(memory_space=pltpu.SMEM),
            pl.BlockSpec((tq, wl), lambda b, hg, i: (b * nq + i, hg)),
            pl.BlockSpec((seq, wl), lambda b, hg, i: (b, hg)),
            pl.BlockSpec((seq, wl), lambda b, hg, i: (b, hg)),
        ],
        out_specs=pl.BlockSpec((tq, wl), lambda b, hg, i: (b * nq + i, hg)),
        out_shape=jax.ShapeDtypeStruct((n, da), F32),
        scratch_shapes=[pltpu.VMEM((groups, rows, 1), F32), pltpu.VMEM((groups, rows, LANES), F32)],
        compiler_params=_cparams("parallel", "parallel", "arbitrary"),
        name="attn_prompt",
    )(sb_bias, q, k_bf, v_bf)


def _attn_sample_kernel(pt_ref, bias_ref, q_ref, kn_ref, vn_ref, *rest, n_pages, page, t_new, heads, dh,
                        scale):
    del pt_ref
    k_pages = rest[:n_pages]
    v_pages = rest[n_pages:2 * n_pages]
    o_ref, r_scr, acc_scr = rest[2 * n_pages:]
    j = pl.program_id(1)
    ht = heads * t_new
    q = q_ref[...] * scale
    qh = [q[:, h * dh:(h + 1) * dh].astype(BF16) for h in range(heads)]
    u_tri = _suffix_matrix(page)

    def tile(k_heads, v_heads, valid, slots):
        z = jnp.concatenate(
            [_dot_nt(qh[h], k_heads[h].astype(BF16)) + bias_ref[0, h] for h in range(heads)], axis=0)
        sp = _softplus(z)
        lk = -sp if valid is None else jnp.where(valid, -sp, 0.0)
        hi = lk.astype(BF16)
        lo = (lk - hi.astype(F32)).astype(BF16)
        cols = [slice(p * page, (p + 1) * page) for p in range(slots)]
        later = _dot(jnp.concatenate([jnp.concatenate([hi[:, c], lo[:, c]], axis=1) for c in cols], axis=0),
                     u_tri)
        r_run = r_scr[...]
        shift = []
        for p, c in enumerate(cols):
            shift.append(later[p * ht:(p + 1) * ht] + r_run)
            r_run = r_run + jnp.sum(lk[:, c], axis=1, keepdims=True)
        r_scr[...] = r_run
        a = jnp.exp(z - sp + jnp.concatenate(shift, axis=1))
        if valid is not None:
            a = jnp.where(valid, a, 0.0)
        acc_scr[...] += jnp.concatenate(
            [_dot(a[h * t_new:(h + 1) * t_new].astype(BF16), v_heads[h].astype(BF16))
             for h in range(heads)], axis=0)

    @pl.when(j == 0)
    def _():
        r_scr[...] = jnp.zeros_like(r_scr)
        acc_scr[...] = jnp.zeros_like(acc_scr)
        pad = jnp.zeros((page - t_new, dh), F32)
        kn = kn_ref[...]
        vn = vn_ref[...]
        valid_new = _iota2((ht, page), 1) < (_iota2((ht, page), 0) % t_new)
        tile([jnp.concatenate([kn[:, h * dh:(h + 1) * dh], pad], axis=0) for h in range(heads)],
             [jnp.concatenate([vn[:, h * dh:(h + 1) * dh], pad], axis=0) for h in range(heads)],
             valid_new, 1)

    def head_rows(refs, h):
        return jnp.concatenate([r[pl.ds(h, page, stride=heads), :] for r in refs], axis=0)

    tile([head_rows(k_pages, h) for h in range(heads)], [head_rows(v_pages, h) for h in range(heads)],
         None, n_pages)

    @pl.when(j == pl.num_programs(1) - 1)
    def _():
        o_ref[...] = jnp.concatenate(
            [acc_scr[h * t_new:(h + 1) * t_new, :] for h in range(heads)], axis=1)


SAMPLE_PAGES_PER_STEP = 16


def _attn_sample(q, k_new, v_new, cache_k, cache_v, layer, page_table, sb_bias, t_new):
    n, da = q.shape
    db, n_pages = page_table.shape
    n_even, n_pool, page, heads, dh = cache_k.shape
    cache_k = cache_k.reshape(n_even, n_pool, page * heads, dh)
    cache_v = cache_v.reshape(n_even, n_pool, page * heads, dh)
    pps = SAMPLE_PAGES_PER_STEP if n_pages % SAMPLE_PAGES_PER_STEP == 0 else n_pages
    steps = n_pages // pps
    page_specs = [
        pl.BlockSpec((None, None, page * heads, dh),
                     lambda b, j, pt, p=p: (layer, pt[b, n_pages - 1 - (j * pps + p)], 0, 0))
        for p in range(pps)
    ]
    row_spec = pl.BlockSpec((t_new, da), lambda b, j, pt: (b, 0))
    grid_spec = pltpu.PrefetchScalarGridSpec(
        num_scalar_prefetch=1,
        grid=(db, steps),
        in_specs=[pl.BlockSpec(memory_space=pltpu.SMEM), row_spec, row_spec, row_spec]
        + page_specs + page_specs,
        out_specs=row_spec,
        scratch_shapes=[pltpu.VMEM((heads * t_new, 1), F32), pltpu.VMEM((heads * t_new, dh), F32)],
    )
    return pl.pallas_call(
        functools.partial(_attn_sample_kernel, n_pages=pps, page=page, t_new=t_new, heads=heads,
                          dh=dh, scale=dh ** -0.5),
        grid_spec=grid_spec,
        out_shape=jax.ShapeDtypeStruct((n, da), F32),
        compiler_params=_cparams("parallel", "arbitrary"),
        name="attn_sample",
    )(page_table, sb_bias, q, k_new, v_new, *([cache_k] * pps), *([cache_v] * pps))


def _even_out_tail(x, oa, diffs, wp_ref, ps_ref, wo_ref):
    da = oa.shape[1]
    y = _dot(oa.astype(BF16), wo_ref[0:da, :])
    for g, diff in enumerate(diffs):
        cb = diff.shape[1]
        ob = _dot(diff.astype(BF16), wp_ref[g]) * ps_ref[:, g * cb:(g + 1) * cb]
        y = y + _dot(ob.astype(BF16), wo_ref[da + g * cb:da + (g + 1) * cb, :])
    return x + y


def _even_out_prompt_kernel(x_ref, oa_ref, u_ref, halo_ref, wp_ref, ps_ref, wo_ref, o_ref, pad_scr, *,
                            tp, tiles_per_seq, cb):
    it = pl.program_id(0) % tiles_per_seq
    pad_scr[0:POOL_HALO, :] = jnp.where(it == 0, 0.0, halo_ref[...])
    pad_scr[POOL_HALO:POOL_HALO + tp, :] = u_ref[...]
    pos = it * tp + _iota2((tp, 1), 0)
    diffs = []
    for g, w in enumerate(POOL_WINDOWS):
        cols = slice(g * cb, (g + 1) * cb)
        s = pad_scr[POOL_HALO:POOL_HALO + tp, cols]
        for j in range(1, w):
            s = s + pad_scr[POOL_HALO - j:POOL_HALO - j + tp, cols]
        cnt = jnp.minimum(w, pos + 1).astype(F32)
        diffs.append(s / cnt - u_ref[:, cols])
    o_ref[...] = _even_out_tail(x_ref[...], oa_ref[...], diffs, wp_ref, ps_ref, wo_ref)


def _even_out_prompt(x, oa, u, w_pool, pool_scale, w_out, layer, seq):
    n, d = x.shape
    db_ = u.shape[1]
    tp = _tile(seq, 512)
    tiles_per_seq = seq // tp
    hb = tp // POOL_HALO
    g_b = w_pool.shape[1]
    cb = db_ // g_b
    return pl.pallas_call(
        functools.partial(_even_out_prompt_kernel, tp=tp, tiles_per_seq=tiles_per_seq, cb=cb),
        grid=(n // tp,),
        in_specs=[
            pl.BlockSpec((tp, d), lambda i: (i, 0)),
            pl.BlockSpec((tp, oa.shape[1]), lambda i: (i, 0)),
            pl.BlockSpec((tp, db_), lambda i: (i, 0)),
            pl.BlockSpec((POOL_HALO, db_), lambda i: (jnp.maximum(i * hb - 1, 0), 0)),
            _resident((None, g_b, cb, cb), lambda i: (layer, 0, 0, 0)),
            pl.BlockSpec((None, 1, db_), lambda i: (layer, 0, 0)),
            _resident((None, d, d), lambda i: (layer, 0, 0)),
        ],
        out_specs=pl.BlockSpec((tp, d), lambda i: (i, 0)),
        out_shape=jax.ShapeDtypeStruct((n, d), F32),
        scratch_shapes=[pltpu.VMEM((POOL_HALO + tp, db_), F32)],
        compiler_params=_cparams("parallel"),
        name="even_out_prompt",
    )(x, oa, u, u, w_pool, pool_scale, w_out)


def _even_out_sample_kernel(x_ref, oa_ref, p_ref, wp_ref, ps_ref, wo_ref, o_ref, *, bb, t_new, pos0, cb):
    pos = pos0 + _iota2((1, t_new, 1), 1)
    diffs = []
    for g, w in enumerate(POOL_WINDOWS):
        cols = slice(g * cb, (g + 1) * cb)
        u3 = p_ref[:, POOL_HALO:POOL_HALO + t_new, cols]
        s = u3
        for j in range(1, w):
            s = s + p_ref[:, POOL_HALO - j:POOL_HALO - j + t_new, cols]
        cnt = jnp.minimum(w, pos + 1).astype(F32)
        diffs.append((s / cnt - u3).reshape(bb * t_new, cb))
    o_ref[...] = _even_out_tail(x_ref[...], oa_ref[...], diffs, wp_ref, ps_ref, wo_ref)


def _even_out_sample(x, oa, padded, w_pool, pool_scale, w_out, layer, t_new, pos0):
    n, d = x.shape
    nb, rows, db_ = padded.shape
    bb = max(b for b in range(1, min(nb, 64) + 1) if nb % b == 0)
    g_b = w_pool.shape[1]
    cb = db_ // g_b
    return pl.pallas_call(
        functools.partial(_even_out_sample_kernel, bb=bb, t_new=t_new, pos0=pos0, cb=cb),
        grid=(nb // bb,),
        in_specs=[
            pl.BlockSpec((bb * t_new, d), lambda i: (i, 0)),
            pl.BlockSpec((bb * t_new, oa.shape[1]), lambda i: (i, 0)),
            pl.BlockSpec((bb, rows, db_), lambda i: (i, 0, 0)),
            _resident((None, g_b, cb, cb), lambda i: (layer, 0, 0, 0)),
            pl.BlockSpec((None, 1, db_), lambda i: (layer, 0, 0)),
            _resident((None, d, d), lambda i: (layer, 0, 0)),
        ],
        out_specs=pl.BlockSpec((bb * t_new, d), lambda i: (i, 0)),
        out_shape=jax.ShapeDtypeStruct((n, d), F32),
        compiler_params=_cparams("parallel"),
        name="even_out_sample",
    )(x, oa, padded, w_pool, pool_scale, w_out)


CONV_PAD = SUBLANES


def _bdot(a, b):
    return lax.dot_general(a, b, (((2,), (1,)), ((0,), (0,))), preferred_element_type=F32)


def _bdot_nt(a, b):
    return lax.dot_general(a, b, (((2,), (2,)), ((0,), (0,))), preferred_element_type=F32)


def _bdot_tn(a, b):
    return lax.dot_general(a, b, (((1,), (1,)), ((0,), (0,))), preferred_element_type=F32)


def _unit_lower_inverse(m, eye):
    n = m.shape[-1]
    p = eye - m
    x = m
    k = 2
    while k < n:
        xb = x.astype(BF16)
        x = _bdot(xb, xb)
        p = p + _bdot(p.astype(BF16), x.astype(BF16))
        k *= 2
    return p


def _gdn_kernel(alog_ref, dtb_ref, qkv_ref, z_ref, ab_ref, cbuf_ref, s0_ref, cw_ref, gn_ref,
                o_ref, sfin_ref, xpad, q_scr, k_scr, v_scr, gc_scr, df_scr, be_scr, s_scr, *,
                bb, tb, chunk, heads, dk, dv, cw):
    t = pl.program_id(1)
    tail = cw - 1
    rows_all = bb * tb
    nb = bb * heads

    @pl.when(t == 0)
    def _():
        s_scr[...] = s0_ref[...]
        xpad[:, CONV_PAD - tail:CONV_PAD, :] = cbuf_ref[...]

    xpad[:, CONV_PAD:CONV_PAD + tb, :] = qkv_ref[...]

    def conv_silu(c0, width):
        acc = None
        for i in range(cw):
            r0 = CONV_PAD - tail + i
            term = xpad[:, r0:r0 + tb, c0:c0 + width] * cw_ref[i:i + 1, c0:c0 + width]
            acc = term if acc is None else acc + term
        return _silu(acc)

    def l2n(x):
        return x * lax.rsqrt(jnp.sum(x * x, axis=-1, keepdims=True) + EPS)

    for h in range(heads):
        q_scr[:, h] = l2n(conv_silu(h * dk, dk)) * (dk ** -0.5)
        k_scr[:, h] = l2n(conv_silu(heads * dk + h * dk, dk))
        v_scr[:, h] = conv_silu(2 * heads * dk + h * dv, dv)

    xpad[:, CONV_PAD - tail:CONV_PAD, :] = xpad[:, CONV_PAD + tb - tail:CONV_PAD + tb, :]

    ab = ab_ref[...].reshape(rows_all, LANES)
    g = -jnp.exp(alog_ref[...]) * _softplus(ab + dtb_ref[...])
    beta = jax.nn.sigmoid(ab)
    rr = _iota2((rows_all, rows_all), 0)
    cc = _iota2((rows_all, rows_all), 1)
    l_blk = jnp.where((rr // chunk == cc // chunk) & (rr >= cc), 1.0, 0.0).astype(BF16)
    after = (_iota2((rows_all, LANES), 0) % chunk) > _iota2((rows_all, LANES), 1)
    for h in range(heads):
        g_rep = jnp.broadcast_to(g[:, h:h + 1], (rows_all, LANES))
        rhs = jnp.concatenate([g_rep, jnp.where(after, g_rep, 0.0)], axis=1)
        hi = rhs.astype(BF16)
        lo = (rhs - hi.astype(F32)).astype(BF16)
        res = _dot(l_blk, hi) + _dot(l_blk, lo)
        gc_scr[:, h] = res[:, 0:LANES].reshape(bb, tb, LANES)
        df_scr[:, h] = res[:, LANES:2 * LANES].reshape(bb, tb, LANES)
        be_scr[:, h] = jnp.broadcast_to(beta[:, heads + h:heads + h + 1], (rows_all, LANES)).reshape(
            bb, tb, LANES)

    row = _iota2((chunk, chunk), 0)
    col = _iota2((chunk, chunk), 1)
    lower = row >= col
    strict = row > col
    eye = jnp.where(row == col, 1.0, 0.0)

    def do_chunk(c, _):
        r0 = pl.multiple_of(c * chunk, chunk)
        rows = pl.ds(r0, chunk)

        def ld(ref):
            return ref[:, :, rows, :].reshape(nb, chunk, LANES)

        q, k, v, gc, df, be = ld(q_scr), ld(k_scr), ld(v_scr), ld(gc_scr), ld(df_scr), ld(be_scr)
        kb = k * be
        decay = jnp.where(lower, jnp.exp(jnp.where(lower, df[:, :, 0:chunk], 0.0)), 0.0)
        kq = _bdot_nt(jnp.concatenate([kb, q], axis=1).astype(BF16), k.astype(BF16))
        m = jnp.where(strict, kq[:, 0:chunk] * decay, 0.0)
        attn = kq[:, chunk:2 * chunk] * decay
        t_inv = _unit_lower_inverse(m, eye)
        eg = jnp.exp(gc)
        sol = _bdot(t_inv.astype(BF16), jnp.concatenate([v * be, kb * eg], axis=2).astype(BF16))
        u_c = sol[:, :, 0:dv]
        w_c = sol[:, :, dv:dv + dk]
        gc_last = gc[:, chunk - 1:chunk, :]
        k_tail = k * jnp.exp(gc_last - gc)
        s = s_scr[...].reshape(nb, dk, dv)
        ws = _bdot(jnp.concatenate([w_c, q * eg], axis=1).astype(BF16), s.astype(BF16))
        v_new = u_c - ws[:, 0:chunk]
        v_new_b = v_new.astype(BF16)
        o = ws[:, chunk:2 * chunk] + _bdot(attn.astype(BF16), v_new_b)
        s_new = s * jnp.exp(gc_last) + _bdot_tn(k_tail.astype(BF16), v_new_b)
        s_scr[...] = s_new.reshape(bb, heads, dk, dv)
        zg = jnp.stack([z_ref[:, rows, h * dv:(h + 1) * dv] for h in range(heads)], axis=1)
        og = (_rms(o, gn_ref[...]) * _silu(zg.reshape(nb, chunk, dv))).reshape(bb, heads, chunk, dv)
        for h in range(heads):
            o_ref[:, rows, h * dv:(h + 1) * dv] = og[:, h]
        return 0

    lax.fori_loop(0, tb // chunk, do_chunk, 0)

    @pl.when(t == pl.num_programs(1) - 1)
    def _():
        sfin_ref[...] = s_scr[...]


def _gdn(qkv, z, ab, conv_buf, s0, conv_w, a_log_row, dt_bias_row, gn_w, layer, batch, seq, chunk_pref,
         bb):
    n, dconv = qkv.shape
    _, heads, dk, dv = s0.shape
    assert dk == LANES and dv == LANES
    cw = conv_w.shape[1]
    tb = _tile(seq, 256)
    chunk = _tile(tb, chunk_pref)
    nt = seq // tb
    dz = heads * dv
    assert batch % bb == 0
    row_map = lambda b, t: (b, t, 0)
    state_map = lambda b, t: (b, 0, 0, 0)
    og, s_new = pl.pallas_call(
        functools.partial(_gdn_kernel, bb=bb, tb=tb, chunk=chunk, heads=heads, dk=dk, dv=dv, cw=cw),
        grid=(batch // bb, nt),
        in_specs=[
            pl.BlockSpec((None, 1, LANES), lambda b, t: (layer, 0, 0)),
            pl.BlockSpec((None, 1, LANES), lambda b, t: (layer, 0, 0)),
            pl.BlockSpec((bb, tb, dconv), row_map),
            pl.BlockSpec((bb, tb, dz), row_map),
            pl.BlockSpec((bb, tb, LANES), row_map),
            pl.BlockSpec((bb, cw - 1, dconv), lambda b, t: (b, 0, 0)),
            pl.BlockSpec((bb, heads, dk, dv), state_map),
            pl.BlockSpec((None, cw, dconv), lambda b, t: (layer, 0, 0)),
            pl.BlockSpec((None, 1, dv), lambda b, t: (layer, 0, 0)),
        ],
        out_specs=[
            pl.BlockSpec((bb, tb, dz), row_map),
            pl.BlockSpec((bb, heads, dk, dv), state_map),
        ],
        out_shape=[
            jax.ShapeDtypeStruct((batch, seq, dz), F32),
            jax.ShapeDtypeStruct((batch, heads, dk, dv), F32),
        ],
        scratch_shapes=[pltpu.VMEM((bb, CONV_PAD + tb, dconv), F32)]
        + [pltpu.VMEM((bb, heads, tb, LANES), F32)] * 6
        + [pltpu.VMEM((bb, heads, dk, dv), F32)],
        compiler_params=_cparams("parallel", "arbitrary"),
        name="gdn",
    )(a_log_row, dt_bias_row, qkv.reshape(batch, seq, dconv), z.reshape(batch, seq, dz),
      ab.reshape(batch, seq, LANES), conv_buf, s0, conv_w, gn_w)
    return og.reshape(n, dz), s_new


def _out_proj_kernel(x_ref, y_ref, w_ref, o_ref):
    o_ref[...] = x_ref[...] + _dot(y_ref[...].astype(BF16), w_ref[...])


def _out_proj(x, y, w, layer):
    n, d = x.shape
    tm = _tile(n, 512)
    return pl.pallas_call(
        _out_proj_kernel,
        grid=(n // tm,),
        in_specs=[
            pl.BlockSpec((tm, d), lambda i: (i, 0)),
            pl.BlockSpec((tm, y.shape[1]), lambda i: (i, 0)),
            _resident((None, y.shape[1], d), lambda i: (layer, 0, 0)),
        ],
        out_specs=pl.BlockSpec((tm, d), lambda i: (i, 0)),
        out_shape=jax.ShapeDtypeStruct((n, d), F32),
        compiler_params=_cparams("parallel"),
        name="out_proj",
    )(x, y, w)


def _trunk(x, batch, seq, pos0, cache, pool_bufs, conv_bufs, rec_states, w):
    depth = w["norm_mix"].shape[0]
    heads_a, dh = w["heads_a"], w["dh"]
    da = heads_a * dh
    new_k, new_v, new_pool, new_conv, new_rec = [], [], [], [], []
    for layer in range(depth):
        i = layer // 2
        x = _ffn(x, w["norm_ffn"], w["w_ffn_in"], w["w_ffn_out"], layer, 0)
        if layer % 2 == 0:
            db_ = w["w_in_even"].shape[2] - 3 * da
            segs = [(0, da, (0,)), (da, 2 * da, (1,)), (2 * da, 3 * da, (2,)), (3 * da, 3 * da + db_, (3,))]
            dts = [F32, F32, F32, F32]
            wds = [da, da, da, db_]
            if cache is None:
                segs[1] = (da, 2 * da, (1, 4))
                segs[2] = (2 * da, 3 * da, (2, 5))
                dts += [BF16, BF16]
                wds += [da, da]
            outs = _proj(x, (w["norm_mix"], layer), (w["w_in_even"], i), segs, dts, wds, "in_even")
            q, k, v, u = outs[:4]
            if cache is None:
                oa = _attn_prompt(q, outs[4], outs[5], w["sb_bias"][i][None], batch, seq, dh)
                x = _even_out_prompt(x, oa, u, w["w_pool"], w["pool_scale"], w["w_out_even"], i, seq)
                new_pool.append(u.reshape(batch, seq, db_)[:, seq - (POOL_HALO - 1):])
            else:
                cache_k, cache_v, page_table = cache
                oa = _attn_sample(q, k, v, cache_k, cache_v, i, page_table, w["sb_bias"][i][None], seq)
                u3 = u.reshape(batch, seq, db_)
                buf = pool_bufs[i]
                padded = jnp.concatenate(
                    [jnp.zeros((batch, POOL_HALO - buf.shape[1], db_), F32), buf, u3], axis=1)
                x = _even_out_sample(x, oa, padded, w["w_pool"], w["pool_scale"], w["w_out_even"], i,
                                     seq, pos0)
                new_pool.append(padded[:, padded.shape[1] - buf.shape[1]:])
            new_k.append(k.reshape(batch, seq, heads_a, dh))
            new_v.append(v.reshape(batch, seq, heads_a, dh))
        else:
            heads_c, dk, dv = rec_states.shape[2:]
            dconv = heads_c * (2 * dk + dv)
            dz = heads_c * dv
            segs = [(0, dconv, (0,)), (dconv, dconv + dz, (1,)), (dconv + dz, dconv + dz + LANES, (2,))]
            qkv, z, ab = _proj(x, (w["norm_mix"], layer), (w["w_in_odd"], i), segs, [F32] * 3,
                               [dconv, dz, LANES], "in_odd")
            og, s_new = _gdn(qkv, z, ab, conv_bufs[i], rec_states[i], w["conv_w"], w["a_log"],
                             w["dt_bias"], w["gn_w"], i, batch, seq, 64, 1 if cache is None else 4)
            x = _out_proj(x, og, w["w_out_odd"], i)
            tail = conv_bufs.shape[2]
            xp = jnp.concatenate([conv_bufs[i], qkv.reshape(batch, seq, dconv)], axis=1)
            new_conv.append(xp[:, xp.shape[1] - tail:])
            new_rec.append(s_new)
        final_g = w["norm_final"] if layer == depth - 1 else None
        x = _ffn(x, w["norm_ffn"], w["w_ffn_in"], w["w_ffn_out"], layer, 1, final_g)
    return x, jnp.stack(new_k), jnp.stack(new_v), jnp.stack(new_pool), jnp.stack(new_conv), jnp.stack(new_rec)


def kernel(x_prompt, x_sample, cache_k, cache_v, page_table, state_pool, state_conv, state_rec, norm_ffn,
           w_ffn_in, w_ffn_out, norm_mix, w_in_even, sb_bias, w_pool, pool_scale, w_out_even, w_in_odd,
           conv_w, a_log, dt_bias, gn_w, w_out_odd, norm_final):
    bp, seq, d = x_prompt.shape
    db, dseq, _ = x_sample.shape
    n_even, n_pool, page, heads_a, dh = cache_k.shape
    n_odd, _, heads_c, dk, dv = state_rec.shape
    da = heads_a * dh
    dconv = heads_c * (2 * dk + dv)
    dz = heads_c * dv
    depth = norm_mix.shape[0]

    def lane_row(p):
        return jnp.pad(p.astype(F32), ((0, 0), (0, LANES - p.shape[1])))[:, None, :]

    w_odd = jnp.concatenate(
        [w_in_odd, jnp.zeros((n_odd, d, LANES - 2 * heads_c), w_in_odd.dtype)], axis=2).astype(BF16)
    w = dict(
        heads_a=heads_a, dh=dh,
        norm_ffn=norm_ffn.reshape(depth, 2, 1, d),
        w_ffn_in=w_ffn_in.astype(BF16),
        w_ffn_out=w_ffn_out.astype(BF16),
        norm_mix=norm_mix.reshape(depth, 1, d),
        w_in_even=w_in_even.astype(BF16),
        sb_bias=sb_bias.astype(F32),
        w_pool=w_pool.astype(BF16),
        pool_scale=pool_scale.reshape(n_even, 1, -1),
        w_out_even=w_out_even.astype(BF16),
        w_in_odd=w_odd,
        conv_w=conv_w,
        a_log=lane_row(a_log),
        dt_bias=lane_row(dt_bias),
        gn_w=gn_w.reshape(n_odd, 1, dv),
        w_out_odd=w_out_odd.astype(BF16),
        norm_final=norm_final.reshape(1, d),
    )
    past_len = page_table.shape[1] * page

    yp, kp, vp, poolp, convp, recp = _trunk(
        x_prompt.reshape(bp * seq, d), bp, seq, 0, None,
        None,
        jnp.zeros((n_odd, bp, conv_w.shape[1] - 1, dconv), state_conv.dtype),
        jnp.zeros((n_odd, bp, heads_c, dk, dv), state_rec.dtype), w)
    ys, ks, vs, pools, convs, recs = _trunk(
        x_sample.reshape(db * dseq, d), db, dseq, past_len, (cache_k, cache_v, page_table),
        state_pool, state_conv, state_rec, w)
    return (yp.reshape(bp, seq, d), ys.reshape(db, dseq, d), kp, vp, ks, vs, poolp, pools, convp, convs,
            recp, recs)
```

```python
import functools

import jax
import jax.numpy as jnp
from jax import lax
from jax.experimental import pallas as pl
from jax.experimental.pallas import tpu as pltpu

F32 = jnp.float32
BF16 = jnp.bfloat16
EPS = 1e-6
FFN_HALF = 0.5
POOL_WINDOWS = (2, 4, 8, 16)
POOL_HALO = 16
LANES = 128
SUBLANES = 8
VMEM_LIMIT = 56 * 1024 * 1024


def _cparams(*sem):
    return pltpu.CompilerParams(dimension_semantics=sem, vmem_limit_bytes=VMEM_LIMIT)


def _tile(n, pref):
    t = pref
    while t > SUBLANES and n % t:
        t //= 2
    assert n % t == 0, (n, pref)
    return t


def _dot(a, b):
    return jnp.dot(a, b, preferred_element_type=F32)


def _dot_nt(a, b):
    return lax.dot_general(a, b, (((1,), (1,)), ((), ())), preferred_element_type=F32)


def _rms(x, g):
    return x * lax.rsqrt(jnp.mean(x * x, axis=-1, keepdims=True) + EPS) * g


def _softplus(x):
    return jnp.maximum(x, 0.0) + jnp.log(1.0 + jnp.exp(-jnp.abs(x)))


def _silu(x):
    return x * jax.nn.sigmoid(x)


def _iota2(shape, dim):
    return lax.broadcasted_iota(jnp.int32, shape, dim)


def _resident(shape, index_map):
    return pl.BlockSpec(shape, index_map, pipeline_mode=pl.Buffered(1))


def _ffn_kernel(x_ref, g_ref, wi_ref, wo_ref, *rest, d_ff, ck, final):
    o_ref = rest[-1]
    x = x_ref[...]
    h = _rms(x, g_ref[...]).astype(BF16)
    acc = None
    for c in range(d_ff // ck):
        gate = _dot(h, wi_ref[:, c * ck:(c + 1) * ck])
        up = _dot(h, wi_ref[:, d_ff + c * ck:d_ff + (c + 1) * ck])
        act = (_silu(gate) * up).astype(BF16)
        part = _dot(act, wo_ref[c * ck:(c + 1) * ck, :])
        acc = part if acc is None else acc + part
    y = x + FFN_HALF * acc
    if final:
        y = _rms(y, rest[0][...])
    o_ref[...] = y


def _ffn(x, norm_ffn, w_in, w_out, layer, j, final_g=None):
    n, d = x.shape
    d_ff = w_out.shape[2]
    tm = _tile(n, 512)
    ck = 256 if d_ff % 256 == 0 else d_ff
    in_specs = [
        pl.BlockSpec((tm, d), lambda i: (i, 0)),
        pl.BlockSpec((None, None, 1, d), lambda i: (layer, j, 0, 0)),
        _resident((None, None, d, 2 * d_ff), lambda i: (layer, j, 0, 0)),
        _resident((None, None, d_ff, d), lambda i: (layer, j, 0, 0)),
    ]
    args = [x, norm_ffn, w_in, w_out]
    if final_g is not None:
        in_specs.append(pl.BlockSpec((1, d), lambda i: (0, 0)))
        args.append(final_g)
    return pl.pallas_call(
        functools.partial(_ffn_kernel, d_ff=d_ff, ck=ck, final=final_g is not None),
        grid=(n // tm,),
        in_specs=in_specs,
        out_specs=pl.BlockSpec((tm, d), lambda i: (i, 0)),
        out_shape=jax.ShapeDtypeStruct((n, d), F32),
        compiler_params=_cparams("parallel"),
        name="ffn",
    )(*args)


def _proj_kernel(x_ref, g_ref, w_ref, *o_refs, segs):
    h = _rms(x_ref[...], g_ref[...]).astype(BF16)
    for (a, b, outs) in segs:
        y = _dot(h, w_ref[:, a:b])
        for oi in outs:
            o_refs[oi][...] = y.astype(o_refs[oi].dtype)


def _proj(x, g, w, segs, out_dtypes, out_widths, name):
    n, d = x.shape
    tm = _tile(n, 512)
    g_arr, g_idx = g
    w_arr, w_idx = w
    in_specs = [
        pl.BlockSpec((tm, d), lambda i: (i, 0)),
        pl.BlockSpec((None, 1, d), lambda i: (g_idx, 0, 0)),
        _resident((None, d, w_arr.shape[2]), lambda i: (w_idx, 0, 0)),
    ]
    return pl.pallas_call(
        functools.partial(_proj_kernel, segs=segs),
        grid=(n // tm,),
        in_specs=in_specs,
        out_specs=[pl.BlockSpec((tm, wd), lambda i: (i, 0)) for wd in out_widths],
        out_shape=[jax.ShapeDtypeStruct((n, wd), dt) for wd, dt in zip(out_widths, out_dtypes)],
        compiler_params=_cparams("parallel"),
        name=name,
    )(x, g_arr, w_arr)


def _suffix_matrix(n):
    row = _iota2((2 * n, n), 0) % n
    col = _iota2((2 * n, n), 1)
    return jnp.where(row > col, 1.0, 0.0).astype(BF16)


ATTN_LANE_BLOCKS = 2


def _attn_prompt_kernel(bias_ref, q_ref, k_ref, v_ref, o_ref, r_scr, acc_scr, *, tq, dh, scale, groups):
    hg = pl.program_id(1)
    i = pl.program_id(2)
    nh = LANES // dh
    lane = _iota2((tq, LANES), 1)
    heads = [(lane >= e * dh) & (lane < (e + 1) * dh) for e in range(nh)]
    u_tri = _suffix_matrix(tq)
    causal = _iota2((nh * tq, tq), 1) < (_iota2((nh * tq, tq), 0) % tq)
    qs, biases = [], []
    for g in range(groups):
        q = q_ref[:, g * LANES:(g + 1) * LANES] * scale
        qs.append(jnp.concatenate([jnp.where(hm, q, 0.0) for hm in heads], axis=0).astype(BF16))
        biases.append([bias_ref[0, (hg * groups + g) * nh + e] for e in range(nh)])

    def tile(j, valid):
        r0 = pl.multiple_of(j * tq, tq)
        gs = range(groups)
        zs = [_dot_nt(qs[g], k_ref[pl.ds(r0, tq), g * LANES:(g + 1) * LANES]) for g in gs]
        zs = [jnp.concatenate([zs[g][e * tq:(e + 1) * tq] + biases[g][e] for e in range(nh)], axis=0)
              for g in gs]
        sps = [_softplus(z) for z in zs]
        drops = sps if valid is None else [jnp.where(valid, sp, 0.0) for sp in sps]
        his = [d.astype(BF16) for d in drops]
        los = [(d - hi.astype(F32)).astype(BF16) for d, hi in zip(drops, his)]
        laters = [_dot(jnp.concatenate([hi, lo], axis=1), u_tri) for hi, lo in zip(his, los)]
        for g in gs:
            a = jnp.exp(zs[g] - sps[g] - laters[g] - r_scr[g])
            if valid is not None:
                a = jnp.where(valid, a, 0.0)
            acc_scr[g] += _dot(a.astype(BF16), v_ref[pl.ds(r0, tq), g * LANES:(g + 1) * LANES])
            r_scr[g] += jnp.sum(drops[g], axis=1, keepdims=True)

    r_scr[...] = jnp.zeros_like(r_scr)
    acc_scr[...] = jnp.zeros_like(acc_scr)
    tile(i, causal)

    def body(s, c):
        tile(i - 1 - s, None)
        return c

    lax.fori_loop(0, i, body, 0)
    for g in range(groups):
        out = jnp.where(heads[0], acc_scr[g, 0:tq], 0.0)
        for e in range(1, nh):
            out = out + jnp.where(heads[e], acc_scr[g, e * tq:(e + 1) * tq], 0.0)
        o_ref[:, g * LANES:(g + 1) * LANES] = out


def _attn_prompt(q, k_bf, v_bf, sb_bias, batch, seq, dh):
    n, da = q.shape
    tq = _tile(seq, 256)
    nq = seq // tq
    groups = ATTN_LANE_BLOCKS if da % (ATTN_LANE_BLOCKS * LANES) == 0 else 1
    wl = groups * LANES
    rows = LANES // dh * tq
    return pl.pallas_call(
        functools.partial(_attn_prompt_kernel, tq=tq, dh=dh, scale=dh ** -0.5, groups=groups),
        grid=(batch, da // wl, nq),
        in_specs=[
            pl.BlockSpec(memory_space=pltpu.SMEM),
            pl.BlockSpec((tq, wl), lambda b, hg, i: (b * nq + i, hg)),
            pl.BlockSpec((seq, wl), lambda b, hg, i: (b, hg)),
            pl.BlockSpec((seq, wl), lambda b, hg, i: (b, hg)),
        ],
        out_specs=pl.BlockSpec((tq, wl), lambda b, hg, i: (b * nq + i, hg)),
        out_shape=jax.ShapeDtypeStruct((n, da), F32),
        scratch_shapes=[pltpu.VMEM((groups, rows, 1), F32), pltpu.VMEM((groups, rows, LANES), F32)],
        compiler_params=_cparams("parallel", "parallel", "arbitrary"),
        name="attn_prompt",
    )(sb_bias, q, k_bf, v_bf)


def _page_head(page_ref, h):
    page, heads, dh = page_ref.shape
    return page_ref.reshape(page * heads, dh)[pl.ds(h, page, stride=heads), :]


def _attn_sample_kernel(pt_ref, bias_ref, q_ref, kn_ref, vn_ref, *rest, n_pages, page, t_new, heads, dh,
                        scale):
    del pt_ref
    k_pages = rest[:n_pages]
    v_pages = rest[n_pages:2 * n_pages]
    o_ref, r_scr, acc_scr = rest[2 * n_pages:]
    j = pl.program_id(1)
    ht = heads * t_new
    q = q_ref[...] * scale
    qh = [q[:, h * dh:(h + 1) * dh].astype(BF16) for h in range(heads)]
    u_tri = _suffix_matrix(page)

    def tile(k_heads, v_heads, valid, slots):
        z = jnp.concatenate(
            [_dot_nt(qh[h], k_heads[h].astype(BF16)) + bias_ref[0, h] for h in range(heads)], axis=0)
        sp = _softplus(z)
        lk = -sp if valid is None else jnp.where(valid, -sp, 0.0)
        hi = lk.astype(BF16)
        lo = (lk - hi.astype(F32)).astype(BF16)
        cols = [slice(p * page, (p + 1) * page) for p in range(slots)]
        later = _dot(jnp.concatenate([jnp.concatenate([hi[:, c], lo[:, c]], axis=1) for c in cols], axis=0),
                     u_tri)
        r_run = r_scr[...]
        shift = []
        for p, c in enumerate(cols):
            shift.append(later[p * ht:(p + 1) * ht] + r_run)
            r_run = r_run + jnp.sum(lk[:, c], axis=1, keepdims=True)
        r_scr[...] = r_run
        a = jnp.exp(z - sp + jnp.concatenate(shift, axis=1))
        if valid is not None:
            a = jnp.where(valid, a, 0.0)
        acc_scr[...] += jnp.concatenate(
            [_dot(a[h * t_new:(h + 1) * t_new].astype(BF16), v_heads[h].astype(BF16))
             for h in range(heads)], axis=0)

    @pl.when(j == 0)
    def _():
        r_scr[...] = jnp.zeros_like(r_scr)
        acc_scr[...] = jnp.zeros_like(acc_scr)
        pad = jnp.zeros((page - t_new, dh), F32)
        kn = kn_ref[...]
        vn = vn_ref[...]
        valid_new = _iota2((ht, page), 1) < (_iota2((ht, page), 0) % t_new)
        tile([jnp.concatenate([kn[:, h * dh:(h + 1) * dh], pad], axis=0) for h in range(heads)],
             [jnp.concatenate([vn[:, h * dh:(h + 1) * dh], pad], axis=0) for h in range(heads)],
             valid_new, 1)

    def head_rows(refs, h):
        return jnp.concatenate([_page_head(r, h) for r in refs], axis=0)

    tile([head_rows(k_pages, h) for h in range(heads)], [head_rows(v_pages, h) for h in range(heads)],
         None, n_pages)

    @pl.when(j == pl.num_programs(1) - 1)
    def _():
        o_ref[...] = jnp.concatenate(
            [acc_scr[h * t_new:(h + 1) * t_new, :] for h in range(heads)], axis=1)


SAMPLE_PAGES_PER_STEP = 16


def _attn_sample(q, k_new, v_new, cache_k, cache_v, layer, page_table, sb_bias, t_new):
    n, da = q.shape
    db, n_pages = page_table.shape
    _, _, page, heads, dh = cache_k.shape
    pps = SAMPLE_PAGES_PER_STEP if n_pages % SAMPLE_PAGES_PER_STEP == 0 else n_pages
    steps = n_pages // pps
    page_specs = [
        pl.BlockSpec((None, None, page, heads, dh),
                     lambda b, j, pt, p=p: (layer, pt[b, n_pages - 1 - (j * pps + p)], 0, 0, 0))
        for p in range(pps)
    ]
    row_spec = pl.BlockSpec((t_new, da), lambda b, j, pt: (b, 0))
    grid_spec = pltpu.PrefetchScalarGridSpec(
        num_scalar_prefetch=1,
        grid=(db, steps),
        in_specs=[pl.BlockSpec(memory_space=pltpu.SMEM), row_spec, row_spec, row_spec]
        + page_specs + page_specs,
        out_specs=row_spec,
        scratch_shapes=[pltpu.VMEM((heads * t_new, 1), F32), pltpu.VMEM((heads * t_new, dh), F32)],
    )
    return pl.pallas_call(
        functools.partial(_attn_sample_kernel, n_pages=pps, page=page, t_new=t_new, heads=heads,
                          dh=dh, scale=dh ** -0.5),
        grid_spec=grid_spec,
        out_shape=jax.ShapeDtypeStruct((n, da), F32),
        compiler_params=_cparams("parallel", "arbitrary"),
        name="attn_sample",
    )(page_table, sb_bias, q, k_new, v_new, *([cache_k] * pps), *([cache_v] * pps))


def _even_out_tail(x, oa, diffs, wp_ref, ps_ref, wo_ref):
    da = oa.shape[1]
    y = _dot(oa.astype(BF16), wo_ref[0:da, :])
    for g, diff in enumerate(diffs):
        cb = diff.shape[1]
        ob = _dot(diff.astype(BF16), wp_ref[g]) * ps_ref[:, g * cb:(g + 1) * cb]
        y = y + _dot(ob.astype(BF16), wo_ref[da + g * cb:da + (g + 1) * cb, :])
    return x + y


def _even_out_prompt_kernel(x_ref, oa_ref, u_ref, halo_ref, wp_ref, ps_ref, wo_ref, o_ref, pad_scr, *,
                            tp, tiles_per_seq, cb):
    it = pl.program_id(0) % tiles_per_seq
    pad_scr[0:POOL_HALO, :] = jnp.where(it == 0, 0.0, halo_ref[...])
    pad_scr[POOL_HALO:POOL_HALO + tp, :] = u_ref[...]
    pos = it * tp + _iota2((tp, 1), 0)
    diffs = []
    for g, w in enumerate(POOL_WINDOWS):
        cols = slice(g * cb, (g + 1) * cb)
        s = pad_scr[POOL_HALO:POOL_HALO + tp, cols]
        for j in range(1, w):
            s = s + pad_scr[POOL_HALO - j:POOL_HALO - j + tp, cols]
        cnt = jnp.minimum(w, pos + 1).astype(F32)
        diffs.append(s / cnt - u_ref[:, cols])
    o_ref[...] = _even_out_tail(x_ref[...], oa_ref[...], diffs, wp_ref, ps_ref, wo_ref)


def _even_out_prompt(x, oa, u, w_pool, pool_scale, w_out, layer, seq):
    n, d = x.shape
    db_ = u.shape[1]
    tp = _tile(seq, 512)
    tiles_per_seq = seq // tp
    hb = tp // POOL_HALO
    g_b = w_pool.shape[1]
    cb = db_ // g_b
    return pl.pallas_call(
        functools.partial(_even_out_prompt_kernel, tp=tp, tiles_per_seq=tiles_per_seq, cb=cb),
        grid=(n // tp,),
        in_specs=[
            pl.BlockSpec((tp, d), lambda i: (i, 0)),
            pl.BlockSpec((tp, oa.shape[1]), lambda i: (i, 0)),
            pl.BlockSpec((tp, db_), lambda i: (i, 0)),
            pl.BlockSpec((POOL_HALO, db_), lambda i: (jnp.maximum(i * hb - 1, 0), 0)),
            _resident((None, g_b, cb, cb), lambda i: (layer, 0, 0, 0)),
            pl.BlockSpec((None, 1, db_), lambda i: (layer, 0, 0)),
            _resident((None, d, d), lambda i: (layer, 0, 0)),
        ],
        out_specs=pl.BlockSpec((tp, d), lambda i: (i, 0)),
        out_shape=jax.ShapeDtypeStruct((n, d), F32),
        scratch_shapes=[pltpu.VMEM((POOL_HALO + tp, db_), F32)],
        compiler_params=_cparams("parallel"),
        name="even_out_prompt",
    )(x, oa, u, u, w_pool, pool_scale, w_out)


def _even_out_sample_kernel(x_ref, oa_ref, p_ref, wp_ref, ps_ref, wo_ref, o_ref, *, bb, t_new, pos0, cb):
    pos = pos0 + _iota2((1, t_new, 1), 1)
    diffs = []
    for g, w in enumerate(POOL_WINDOWS):
        cols = slice(g * cb, (g + 1) * cb)
        u3 = p_ref[:, POOL_HALO:POOL_HALO + t_new, cols]
        s = u3
        for j in range(1, w):
            s = s + p_ref[:, POOL_HALO - j:POOL_HALO - j + t_new, cols]
        cnt = jnp.minimum(w, pos + 1).astype(F32)
        diffs.append((s / cnt - u3).reshape(bb * t_new, cb))
    o_ref[...] = _even_out_tail(x_ref[...], oa_ref[...], diffs, wp_ref, ps_ref, wo_ref)


def _even_out_sample(x, oa, padded, w_pool, pool_scale, w_out, layer, t_new, pos0):
    n, d = x.shape
    nb, rows, db_ = padded.shape
    bb = max(b for b in range(1, min(nb, 64) + 1) if nb % b == 0)
    g_b = w_pool.shape[1]
    cb = db_ // g_b
    return pl.pallas_call(
        functools.partial(_even_out_sample_kernel, bb=bb, t_new=t_new, pos0=pos0, cb=cb),
        grid=(nb // bb,),
        in_specs=[
            pl.BlockSpec((bb * t_new, d), lambda i: (i, 0)),
            pl.BlockSpec((bb * t_new, oa.shape[1]), lambda i: (i, 0)),
            pl.BlockSpec((bb, rows, db_), lambda i: (i, 0, 0)),
            _resident((None, g_b, cb, cb), lambda i: (layer, 0, 0, 0)),
            pl.BlockSpec((None, 1, db_), lambda i: (layer, 0, 0)),
            _resident((None, d, d), lambda i: (layer, 0, 0)),
        ],
        out_specs=pl.BlockSpec((bb * t_new, d), lambda i: (i, 0)),
        out_shape=jax.ShapeDtypeStruct((n, d), F32),
        compiler_params=_cparams("parallel"),
        name="even_out_sample",
    )(x, oa, padded, w_pool, pool_scale, w_out)


CONV_PAD = SUBLANES


def _bdot(a, b):
    return lax.dot_general(a, b, (((2,), (1,)), ((0,), (0,))), preferred_element_type=F32)


def _bdot_nt(a, b):
    return lax.dot_general(a, b, (((2,), (2,)), ((0,), (0,))), preferred_element_type=F32)


def _bdot_tn(a, b):
    return lax.dot_general(a, b, (((1,), (1,)), ((0,), (0,))), preferred_element_type=F32)


def _unit_lower_inverse(m, eye):
    n = m.shape[-1]
    p = eye - m
    x = m
    k = 2
    while k < n:
        xb = x.astype(BF16)
        x = _bdot(xb, xb)
        p = p + _bdot(p.astype(BF16), x.astype(BF16))
        k *= 2
    return p


def _gdn_kernel(alog_ref, dtb_ref, qkv_ref, z_ref, ab_ref, cbuf_ref, s0_ref, cw_ref, gn_ref,
                o_ref, sfin_ref, xpad, q_scr, k_scr, v_scr, gc_scr, df_scr, be_scr, s_scr, *,
                bb, tb, chunk, heads, dk, dv, cw):
    t = pl.program_id(1)
    tail = cw - 1
    rows_all = bb * tb
    nb = bb * heads

    @pl.when(t == 0)
    def _():
        s_scr[...] = s0_ref[...]
        xpad[:, CONV_PAD - tail:CONV_PAD, :] = cbuf_ref[...]

    xpad[:, CONV_PAD:CONV_PAD + tb, :] = qkv_ref[...]

    def conv_silu(c0, width):
        acc = None
        for i in range(cw):
            r0 = CONV_PAD - tail + i
            term = xpad[:, r0:r0 + tb, c0:c0 + width] * cw_ref[i:i + 1, c0:c0 + width]
            acc = term if acc is None else acc + term
        return _silu(acc)

    def l2n(x):
        return x * lax.rsqrt(jnp.sum(x * x, axis=-1, keepdims=True) + EPS)

    for h in range(heads):
        q_scr[:, h] = l2n(conv_silu(h * dk, dk)) * (dk ** -0.5)
        k_scr[:, h] = l2n(conv_silu(heads * dk + h * dk, dk))
        v_scr[:, h] = conv_silu(2 * heads * dk + h * dv, dv)

    xpad[:, CONV_PAD - tail:CONV_PAD, :] = xpad[:, CONV_PAD + tb - tail:CONV_PAD + tb, :]

    ab = ab_ref[...].reshape(rows_all, LANES)
    g = -jnp.exp(alog_ref[...]) * _softplus(ab + dtb_ref[...])
    beta = jax.nn.sigmoid(ab)
    rr = _iota2((rows_all, rows_all), 0)
    cc = _iota2((rows_all, rows_all), 1)
    l_blk = jnp.where((rr // chunk == cc // chunk) & (rr >= cc), 1.0, 0.0).astype(BF16)
    after = (_iota2((rows_all, LANES), 0) % chunk) > _iota2((rows_all, LANES), 1)
    for h in range(heads):
        g_rep = jnp.broadcast_to(g[:, h:h + 1], (rows_all, LANES))
        rhs = jnp.concatenate([g_rep, jnp.where(after, g_rep, 0.0)], axis=1)
        hi = rhs.astype(BF16)
        lo = (rhs - hi.astype(F32)).astype(BF16)
        res = _dot(l_blk, hi) + _dot(l_blk, lo)
        gc_scr[:, h] = res[:, 0:LANES].reshape(bb, tb, LANES)
        df_scr[:, h] = res[:, LANES:2 * LANES].reshape(bb, tb, LANES)
        be_scr[:, h] = jnp.broadcast_to(beta[:, heads + h:heads + h + 1], (rows_all, LANES)).reshape(
            bb, tb, LANES)

    row = _iota2((chunk, chunk), 0)
    col = _iota2((chunk, chunk), 1)
    lower = row >= col
    strict = row > col
    eye = jnp.where(row == col, 1.0, 0.0)

    def do_chunk(c, _):
        r0 = pl.multiple_of(c * chunk, chunk)
        rows = pl.ds(r0, chunk)

        def ld(ref):
            return ref[:, :, rows, :].reshape(nb, chunk, LANES)

        q, k, v, gc, df, be = ld(q_scr), ld(k_scr), ld(v_scr), ld(gc_scr), ld(df_scr), ld(be_scr)
        kb = k * be
        decay = jnp.where(lower, jnp.exp(jnp.where(lower, df[:, :, 0:chunk], 0.0)), 0.0)
        kq = _bdot_nt(jnp.concatenate([kb, q], axis=1).astype(BF16), k.astype(BF16))
        m = jnp.where(strict, kq[:, 0:chunk] * decay, 0.0)
        attn = kq[:, chunk:2 * chunk] * decay
        t_inv = _unit_lower_inverse(m, eye)
        eg = jnp.exp(gc)
        sol = _bdot(t_inv.astype(BF16), jnp.concatenate([v * be, kb * eg], axis=2).astype(BF16))
        u_c = sol[:, :, 0:dv]
        w_c = sol[:, :, dv:dv + dk]
        gc_last = gc[:, chunk - 1:chunk, :]
        k_tail = k * jnp.exp(gc_last - gc)
        s = s_scr[...].reshape(nb, dk, dv)
        ws = _bdot(jnp.concatenate([w_c, q * eg], axis=1).astype(BF16), s.astype(BF16))
        v_new = u_c - ws[:, 0:chunk]
        v_new_b = v_new.astype(BF16)
        o = ws[:, chunk:2 * chunk] + _bdot(attn.astype(BF16), v_new_b)
        s_new = s * jnp.exp(gc_last) + _bdot_tn(k_tail.astype(BF16), v_new_b)
        s_scr[...] = s_new.reshape(bb, heads, dk, dv)
        zg = jnp.stack([z_ref[:, rows, h * dv:(h + 1) * dv] for h in range(heads)], axis=1)
        og = (_rms(o, gn_ref[...]) * _silu(zg.reshape(nb, chunk, dv))).reshape(bb, heads, chunk, dv)
        for h in range(heads):
            o_ref[:, rows, h * dv:(h + 1) * dv] = og[:, h]
        return 0

    lax.fori_loop(0, tb // chunk, do_chunk, 0)

    @pl.when(t == pl.num_programs(1) - 1)
    def _():
        sfin_ref[...] = s_scr[...]


def _gdn(qkv, z, ab, conv_buf, s0, conv_w, a_log_row, dt_bias_row, gn_w, layer, batch, seq, chunk_pref,
         bb):
    n, dconv = qkv.shape
    _, heads, dk, dv = s0.shape
    assert dk == LANES and dv == LANES
    cw = conv_w.shape[1]
    tb = _tile(seq, 256)
    chunk = _tile(tb, chunk_pref)
    nt = seq // tb
    dz = heads * dv
    assert batch % bb == 0
    row_map = lambda b, t: (b, t, 0)
    state_map = lambda b, t: (b, 0, 0, 0)
    og, s_new = pl.pallas_call(
        functools.partial(_gdn_kernel, bb=bb, tb=tb, chunk=chunk, heads=heads, dk=dk, dv=dv, cw=cw),
        grid=(batch // bb, nt),
        in_specs=[
            pl.BlockSpec((None, 1, LANES), lambda b, t: (layer, 0, 0)),
            pl.BlockSpec((None, 1, LANES), lambda b, t: (layer, 0, 0)),
            pl.BlockSpec((bb, tb, dconv), row_map),
            pl.BlockSpec((bb, tb, dz), row_map),
            pl.BlockSpec((bb, tb, LANES), row_map),
            pl.BlockSpec((bb, cw - 1, dconv), lambda b, t: (b, 0, 0)),
            pl.BlockSpec((bb, heads, dk, dv), state_map),
            pl.BlockSpec((None, cw, dconv), lambda b, t: (layer, 0, 0)),
            pl.BlockSpec((None, 1, dv), lambda b, t: (layer, 0, 0)),
        ],
        out_specs=[
            pl.BlockSpec((bb, tb, dz), row_map),
            pl.BlockSpec((bb, heads, dk, dv), state_map),
        ],
        out_shape=[
            jax.ShapeDtypeStruct((batch, seq, dz), F32),
            jax.ShapeDtypeStruct((batch, heads, dk, dv), F32),
        ],
        scratch_shapes=[pltpu.VMEM((bb, CONV_PAD + tb, dconv), F32)]
        + [pltpu.VMEM((bb, heads, tb, LANES), F32)] * 6
        + [pltpu.VMEM((bb, heads, dk, dv), F32)],
        compiler_params=_cparams("parallel", "arbitrary"),
        name="gdn",
    )(a_log_row, dt_bias_row, qkv.reshape(batch, seq, dconv), z.reshape(batch, seq, dz),
      ab.reshape(batch, seq, LANES), conv_buf, s0, conv_w, gn_w)
    return og.reshape(n, dz), s_new


def _out_proj_kernel(x_ref, y_ref, w_ref, o_ref):
    o_ref[...] = x_ref[...] + _dot(y_ref[...].astype(BF16), w_ref[...])


def _out_proj(x, y, w, layer):
    n, d = x.shape
    tm = _tile(n, 512)
    return pl.pallas_call(
        _out_proj_kernel,
        grid=(n // tm,),
        in_specs=[
            pl.BlockSpec((tm, d), lambda i: (i, 0)),
            pl.BlockSpec((tm, y.shape[1]), lambda i: (i, 0)),
            _resident((None, y.shape[1], d), lambda i: (layer, 0, 0)),
        ],
        out_specs=pl.BlockSpec((tm, d), lambda i: (i, 0)),
        out_shape=jax.ShapeDtypeStruct((n, d), F32),
        compiler_params=_cparams("parallel"),
        name="out_proj",
    )(x, y, w)


def _trunk(x, batch, seq, pos0, cache, pool_bufs, conv_bufs, rec_states, w):
    depth = w["norm_mix"].shape[0]
    heads_a, dh = w["heads_a"], w["dh"]
    da = heads_a * dh
    new_k, new_v, new_pool, new_conv, new_rec = [], [], [], [], []
    for layer in range(depth):
        i = layer // 2
        x = _ffn(x, w["norm_ffn"], w["w_ffn_in"], w["w_ffn_out"], layer, 0)
        if layer % 2 == 0:
            db_ = w["w_in_even"].shape[2] - 3 * da
            segs = [(0, da, (0,)), (da, 2 * da, (1,)), (2 * da, 3 * da, (2,)), (3 * da, 3 * da + db_, (3,))]
            dts = [F32, F32, F32, F32]
            wds = [da, da, da, db_]
            if cache is None:
                segs[1] = (da, 2 * da, (1, 4))
                segs[2] = (2 * da, 3 * da, (2, 5))
                dts += [BF16, BF16]
                wds += [da, da]
            outs = _proj(x, (w["norm_mix"], layer), (w["w_in_even"], i), segs, dts, wds, "in_even")
            q, k, v, u = outs[:4]
            if cache is None:
                oa = _attn_prompt(q, outs[4], outs[5], w["sb_bias"][i][None], batch, seq, dh)
                x = _even_out_prompt(x, oa, u, w["w_pool"], w["pool_scale"], w["w_out_even"], i, seq)
                new_pool.append(u.reshape(batch, seq, db_)[:, seq - (POOL_HALO - 1):])
            else:
                cache_k, cache_v, page_table = cache
                oa = _attn_sample(q, k, v, cache_k, cache_v, i, page_table, w["sb_bias"][i][None], seq)
                u3 = u.reshape(batch, seq, db_)
                buf = pool_bufs[i]
                padded = jnp.concatenate(
                    [jnp.zeros((batch, POOL_HALO - buf.shape[1], db_), F32), buf, u3], axis=1)
                x = _even_out_sample(x, oa, padded, w["w_pool"], w["pool_scale"], w["w_out_even"], i,
                                     seq, pos0)
                new_pool.append(padded[:, padded.shape[1] - buf.shape[1]:])
            new_k.append(k.reshape(batch, seq, heads_a, dh))
            new_v.append(v.reshape(batch, seq, heads_a, dh))
        else:
            heads_c, dk, dv = rec_states.shape[2:]
            dconv = heads_c * (2 * dk + dv)
            dz = heads_c * dv
            segs = [(0, dconv, (0,)), (dconv, dconv + dz, (1,)), (dconv + dz, dconv + dz + LANES, (2,))]
            qkv, z, ab = _proj(x, (w["norm_mix"], layer), (w["w_in_odd"], i), segs, [F32] * 3,
                               [dconv, dz, LANES], "in_odd")
            og, s_new = _gdn(qkv, z, ab, conv_bufs[i], rec_states[i], w["conv_w"], w["a_log"],
                             w["dt_bias"], w["gn_w"], i, batch, seq, 64, 1 if cache is None else 4)
            x = _out_proj(x, og, w["w_out_odd"], i)
            tail = conv_bufs.shape[2]
            xp = jnp.concatenate([conv_bufs[i], qkv.reshape(batch, seq, dconv)], axis=1)
            new_conv.append(xp[:, xp.shape[1] - tail:])
            new_rec.append(s_new)
        final_g = w["norm_final"] if layer == depth - 1 else None
        x = _ffn(x, w["norm_ffn"], w["w_ffn_in"], w["w_ffn_out"], layer, 1, final_g)
    return x, jnp.stack(new_k), jnp.stack(new_v), jnp.stack(new_pool), jnp.stack(new_conv), jnp.stack(new_rec)


def kernel(x_prompt, x_sample, cache_k, cache_v, page_table, state_pool, state_conv, state_rec, norm_ffn,
           w_ffn_in, w_ffn_out, norm_mix, w_in_even, sb_bias, w_pool, pool_scale, w_out_even, w_in_odd,
           conv_w, a_log, dt_bias, gn_w, w_out_odd, norm_final):
    bp, seq, d = x_prompt.shape
    db, dseq, _ = x_sample.shape
    n_even, n_pool, page, heads_a, dh = cache_k.shape
    n_odd, _, heads_c, dk, dv = state_rec.shape
    da = heads_a * dh
    dconv = heads_c * (2 * dk + dv)
    dz = heads_c * dv
    depth = norm_mix.shape[0]

    def lane_row(p):
        return jnp.pad(p.astype(F32), ((0, 0), (0, LANES - p.shape[1])))[:, None, :]

    w_odd = jnp.concatenate(
        [w_in_odd, jnp.zeros((n_odd, d, LANES - 2 * heads_c), w_in_odd.dtype)], axis=2).astype(BF16)
    w = dict(
        heads_a=heads_a, dh=dh,
        norm_ffn=norm_ffn.reshape(depth, 2, 1, d),
        w_ffn_in=w_ffn_in.astype(BF16),
        w_ffn_out=w_ffn_out.astype(BF16),
        norm_mix=norm_mix.reshape(depth, 1, d),
        w_in_even=w_in_even.astype(BF16),
        sb_bias=sb_bias.astype(F32),
        w_pool=w_pool.astype(BF16),
        pool_scale=pool_scale.reshape(n_even, 1, -1),
        w_out_even=w_out_even.astype(BF16),
        w_in_odd=w_odd,
        conv_w=conv_w,
        a_log=lane_row(a_log),
        dt_bias=lane_row(dt_bias),
        gn_w=gn_w.reshape(n_odd, 1, dv),
        w_out_odd=w_out_odd.astype(BF16),
        norm_final=norm_final.reshape(1, d),
    )
    past_len = page_table.shape[1] * page

    yp, kp, vp, poolp, convp, recp = _trunk(
        x_prompt.reshape(bp * seq, d), bp, seq, 0, None,
        None,
        jnp.zeros((n_odd, bp, conv_w.shape[1] - 1, dconv), state_conv.dtype),
        jnp.zeros((n_odd, bp, heads_c, dk, dv), state_rec.dtype), w)
    ys, ks, vs, pools, convs, recs = _trunk(
        x_sample.reshape(db * dseq, d), db, dseq, past_len, (cache_k, cache_v, page_table),
        state_pool, state_conv, state_rec, w)
    return (yp.reshape(bp, seq, d), ys.reshape(db, dseq, d), kp, vp, ks, vs, poolp, pools, convp, convs,
            recp, recs)
```

```python
import functools

import jax
import jax.numpy as jnp
from jax import lax
from jax.experimental import pallas as pl
from jax.experimental.pallas import tpu as pltpu

F32 = jnp.float32
BF16 = jnp.bfloat16
EPS = 1e-6
FFN_HALF = 0.5
POOL_WINDOWS = (2, 4, 8, 16)
POOL_HALO = 16
LANES = 128
SUBLANES = 8
VMEM_LIMIT = 56 * 1024 * 1024
NEG_LOG2E = -1.4426950408889634


def _cparams(*sem):
    return pltpu.CompilerParams(dimension_semantics=sem, vmem_limit_bytes=VMEM_LIMIT)


def _tile(n, pref):
    t = pref
    while t > SUBLANES and n % t:
        t //= 2
    assert n % t == 0, (n, pref)
    return t


def _dot(a, b):
    return jnp.dot(a, b, preferred_element_type=F32)


def _dot_nt(a, b):
    return lax.dot_general(a, b, (((1,), (1,)), ((), ())), preferred_element_type=F32)


def _rms(x, g):
    return x * lax.rsqrt(jnp.mean(x * x, axis=-1, keepdims=True) + EPS) * g


def _softplus(x):
    return jnp.maximum(x, 0.0) + jnp.log(1.0 + jnp.exp2(jnp.abs(x) * NEG_LOG2E))


def _silu(x):
    return x * jax.nn.sigmoid(x)


def _iota2(shape, dim):
    return lax.broadcasted_iota(jnp.int32, shape, dim)


def _resident(shape, index_map):
    return pl.BlockSpec(shape, index_map, pipeline_mode=pl.Buffered(1))


def _ffn_kernel(x_ref, g_ref, wi_ref, wo_ref, *rest, d_ff, ck, final):
    o_ref = rest[-1]
    x = x_ref[...]
    h = _rms(x, g_ref[...]).astype(BF16)
    acc = None
    for c in range(d_ff // ck):
        gate = _dot(h, wi_ref[:, c * ck:(c + 1) * ck])
        up = _dot(h, wi_ref[:, d_ff + c * ck:d_ff + (c + 1) * ck])
        act = (_silu(gate) * up).astype(BF16)
        part = _dot(act, wo_ref[c * ck:(c + 1) * ck, :])
        acc = part if acc is None else acc + part
    y = x + FFN_HALF * acc
    if final:
        y = _rms(y, rest[0][...])
    o_ref[...] = y


def _ffn(x, norm_ffn, w_in, w_out, layer, j, final_g=None):
    n, d = x.shape
    d_ff = w_out.shape[2]
    tm = _tile(n, 512)
    ck = 256 if d_ff % 256 == 0 else d_ff
    in_specs = [
        pl.BlockSpec((tm, d), lambda i: (i, 0)),
        pl.BlockSpec((None, None, 1, d), lambda i: (layer, j, 0, 0)),
        _resident((None, None, d, 2 * d_ff), lambda i: (layer, j, 0, 0)),
        _resident((None, None, d_ff, d), lambda i: (layer, j, 0, 0)),
    ]
    args = [x, norm_ffn, w_in, w_out]
    if final_g is not None:
        in_specs.append(pl.BlockSpec((1, d), lambda i: (0, 0)))
        args.append(final_g)
    return pl.pallas_call(
        functools.partial(_ffn_kernel, d_ff=d_ff, ck=ck, final=final_g is not None),
        grid=(n // tm,),
        in_specs=in_specs,
        out_specs=pl.BlockSpec((tm, d), lambda i: (i, 0)),
        out_shape=jax.ShapeDtypeStruct((n, d), F32),
        compiler_params=_cparams("parallel"),
        name="ffn",
    )(*args)


def _proj_kernel(x_ref, g_ref, w_ref, *o_refs, segs):
    h = _rms(x_ref[...], g_ref[...]).astype(BF16)
    for (a, b, outs) in segs:
        y = _dot(h, w_ref[:, a:b])
        for oi in outs:
            o_refs[oi][...] = y.astype(o_refs[oi].dtype)


def _proj(x, g, w, segs, out_dtypes, out_widths, name):
    n, d = x.shape
    tm = _tile(n, 512)
    g_arr, g_idx = g
    w_arr, w_idx = w
    in_specs = [
        pl.BlockSpec((tm, d), lambda i: (i, 0)),
        pl.BlockSpec((None, 1, d), lambda i: (g_idx, 0, 0)),
        _resident((None, d, w_arr.shape[2]), lambda i: (w_idx, 0, 0)),
    ]
    return pl.pallas_call(
        functools.partial(_proj_kernel, segs=segs),
        grid=(n // tm,),
        in_specs=in_specs,
        out_specs=[pl.BlockSpec((tm, wd), lambda i: (i, 0)) for wd in out_widths],
        out_shape=[jax.ShapeDtypeStruct((n, wd), dt) for wd, dt in zip(out_widths, out_dtypes)],
        compiler_params=_cparams("parallel"),
        name=name,
    )(x, g_arr, w_arr)


def _suffix_matrix(n):
    row = _iota2((2 * n, n), 0) % n
    col = _iota2((2 * n, n), 1)
    return jnp.where(row > col, 1.0, 0.0).astype(BF16)


ATTN_LANE_BLOCKS = 2


def _attn_prompt_kernel(bias_ref, q_ref, k_ref, v_ref, o_ref, r_scr, acc_scr, *, tq, dh, scale, groups):
    hg = pl.program_id(1)
    i = pl.program_id(2)
    nh = LANES // dh
    lane = _iota2((tq, LANES), 1)
    heads = [(lane >= e * dh) & (lane < (e + 1) * dh) for e in range(nh)]
    u_tri = _suffix_matrix(tq)
    causal = _iota2((nh * tq, tq), 1) < (_iota2((nh * tq, tq), 0) % tq)
    qs, biases = [], []
    for g in range(groups):
        q = q_ref[:, g * LANES:(g + 1) * LANES] * scale
        qs.append(jnp.concatenate([jnp.where(hm, q, 0.0) for hm in heads], axis=0).astype(BF16))
        biases.append([bias_ref[0, (hg * groups + g) * nh + e] for e in range(nh)])

    def tile(j, valid):
        r0 = pl.multiple_of(j * tq, tq)
        gs = range(groups)
        zs = [_dot_nt(qs[g], k_ref[pl.ds(r0, tq), g * LANES:(g + 1) * LANES]) for g in gs]
        zs = [jnp.concatenate([zs[g][e * tq:(e + 1) * tq] + biases[g][e] for e in range(nh)], axis=0)
              for g in gs]
        sps = [_softplus(z) for z in zs]
        logsig = [z - sp for z, sp in zip(zs, sps)]
        drops = sps if valid is None else [jnp.where(valid, sp, 0.0) for sp in sps]
        his = [d.astype(BF16) for d in drops]
        los = [(d - hi.astype(F32)).astype(BF16) for d, hi in zip(drops, his)]
        laters = [_dot(jnp.concatenate([hi, lo], axis=1), u_tri) for hi, lo in zip(his, los)]
        for g in gs:
            a = jnp.exp(logsig[g] - laters[g] - r_scr[g])
            if valid is not None:
                a = jnp.where(valid, a, 0.0)
            acc_scr[g] += _dot(a.astype(BF16), v_ref[pl.ds(r0, tq), g * LANES:(g + 1) * LANES])
            r_scr[g] += jnp.sum(drops[g], axis=1, keepdims=True)

    r_scr[...] = jnp.zeros_like(r_scr)
    acc_scr[...] = jnp.zeros_like(acc_scr)
    tile(i, causal)

    def body(s, c):
        tile(i - 1 - s, None)
        return c

    lax.fori_loop(0, i, body, 0)
    for g in range(groups):
        out = jnp.where(heads[0], acc_scr[g, 0:tq], 0.0)
        for e in range(1, nh):
            out = out + jnp.where(heads[e], acc_scr[g, e * tq:(e + 1) * tq], 0.0)
        o_ref[:, g * LANES:(g + 1) * LANES] = out


def _attn_prompt(q, k_bf, v_bf, sb_bias, batch, seq, dh):
    n, da = q.shape
    tq = _tile(seq, 256)
    nq = seq // tq
    groups = ATTN_LANE_BLOCKS if da % (ATTN_LANE_BLOCKS * LANES) == 0 else 1
    wl = groups * LANES
    rows = LANES // dh * tq
    return pl.pallas_call(
        functools.partial(_attn_prompt_kernel, tq=tq, dh=dh, scale=dh ** -0.5, groups=groups),
        grid=(batch, da // wl, nq),
        in_specs=[
            pl.BlockSpec(memory_space=pltpu.SMEM),
            pl.BlockSpec((tq, wl), lambda b, hg, i: (b * nq + i, hg)),
            pl.BlockSpec((seq, wl), lambda b, hg, i: (b, hg)),
            pl.BlockSpec((seq, wl), lambda b, hg, i: (b, hg)),
        ],
        out_specs=pl.BlockSpec((tq, wl), lambda b, hg, i: (b * nq + i, hg)),
        out_shape=jax.ShapeDtypeStruct((n, da), F32),
        scratch_shapes=[pltpu.VMEM((groups, rows, 1), F32), pltpu.VMEM((groups, rows, LANES), F32)],
        compiler_params=_cparams("parallel", "parallel", "arbitrary"),
        name="attn_prompt",
    )(sb_bias, q, k_bf, v_bf)


def _attn_sample_kernel(pt_ref, bias_ref, q_ref, kn_ref, vn_ref, *rest, n_pages, page, t_new, heads, dh,
                        scale):
    del pt_ref
    k_pages = rest[:n_pages]
    v_pages = rest[n_pages:2 * n_pages]
    o_ref, r_scr, acc_scr = rest[2 * n_pages:]
    j = pl.program_id(1)
    ht = heads * t_new
    q = q_ref[...] * scale
    qh = [q[:, h * dh:(h + 1) * dh].astype(BF16) for h in range(heads)]
    u_tri = _suffix_matrix(page)

    def tile(logits, weighted, valid, slots):
        z = jnp.concatenate([logits(h) + bias_ref[0, h] for h in range(heads)], axis=0)
        sp = _softplus(z)
        drop = sp if valid is None else jnp.where(valid, sp, 0.0)
        hi = drop.astype(BF16)
        lo = (drop - hi.astype(F32)).astype(BF16)
        cols = [slice(p * page, (p + 1) * page) for p in range(slots)]
        later = _dot(jnp.concatenate([jnp.concatenate([hi[:, c], lo[:, c]], axis=1) for c in cols], axis=0),
                     u_tri)
        r_run = r_scr[...]
        shift = []
        for p, c in enumerate(cols):
            shift.append(later[p * ht:(p + 1) * ht] + r_run)
            r_run = r_run + jnp.sum(drop[:, c], axis=1, keepdims=True)
        r_scr[...] = r_run
        a = jnp.exp(z - sp - jnp.concatenate(shift, axis=1))
        if valid is not None:
            a = jnp.where(valid, a, 0.0)
        acc_scr[...] += jnp.concatenate(
            [weighted(h, a[h * t_new:(h + 1) * t_new].astype(BF16)) for h in range(heads)], axis=0)

    @pl.when(j == 0)
    def _():
        r_scr[...] = jnp.zeros_like(r_scr)
        acc_scr[...] = jnp.zeros_like(acc_scr)
        pad = jnp.zeros((page - t_new, dh), F32)
        kn = kn_ref[...]
        vn = vn_ref[...]
        valid_new = _iota2((ht, page), 1) < (_iota2((ht, page), 0) % t_new)

        def padded(x, h):
            return jnp.concatenate([x[:, h * dh:(h + 1) * dh], pad], axis=0).astype(BF16)

        tile(lambda h: _dot_nt(qh[h], padded(kn, h)), lambda h, a: _dot(a, padded(vn, h)), valid_new, 1)

    def lanes_of(refs, h):
        return jnp.concatenate([r[h] for r in refs], axis=1).astype(BF16)

    tile(lambda h: _dot(qh[h], lanes_of(k_pages, h)), lambda h, a: _dot_nt(a, lanes_of(v_pages, h)),
         None, n_pages)

    @pl.when(j == pl.num_programs(1) - 1)
    def _():
        o_ref[...] = jnp.concatenate(
            [acc_scr[h * t_new:(h + 1) * t_new, :] for h in range(heads)], axis=1)


SAMPLE_PAGES_PER_STEP = 16


def _attn_sample(q, k_new, v_new, cache_k, cache_v, layer, page_table, sb_bias, t_new):
    n, da = q.shape
    db, n_pages = page_table.shape
    _, _, page, heads, dh = cache_k.shape
    cache_k = jnp.transpose(cache_k, (0, 1, 3, 4, 2))
    cache_v = jnp.transpose(cache_v, (0, 1, 3, 4, 2))
    pps = SAMPLE_PAGES_PER_STEP if n_pages % SAMPLE_PAGES_PER_STEP == 0 else n_pages
    steps = n_pages // pps
    page_specs = [
        pl.BlockSpec((None, None, heads, dh, page),
                     lambda b, j, pt, p=p: (layer, pt[b, n_pages - 1 - (j * pps + p)], 0, 0, 0))
        for p in range(pps)
    ]
    row_spec = pl.BlockSpec((t_new, da), lambda b, j, pt: (b, 0))
    grid_spec = pltpu.PrefetchScalarGridSpec(
        num_scalar_prefetch=1,
        grid=(db, steps),
        in_specs=[pl.BlockSpec(memory_space=pltpu.SMEM), row_spec, row_spec, row_spec]
        + page_specs + page_specs,
        out_specs=row_spec,
        scratch_shapes=[pltpu.VMEM((heads * t_new, 1), F32), pltpu.VMEM((heads * t_new, dh), F32)],
    )
    return pl.pallas_call(
        functools.partial(_attn_sample_kernel, n_pages=pps, page=page, t_new=t_new, heads=heads,
                          dh=dh, scale=dh ** -0.5),
        grid_spec=grid_spec,
        out_shape=jax.ShapeDtypeStruct((n, da), F32),
        compiler_params=_cparams("parallel", "arbitrary"),
        name="attn_sample",
    )(page_table, sb_bias, q, k_new, v_new, *([cache_k] * pps), *([cache_v] * pps))


def _even_out_tail(x, oa, diffs, wp_ref, ps_ref, wo_ref):
    da = oa.shape[1]
    y = _dot(oa.astype(BF16), wo_ref[0:da, :])
    for g, diff in enumerate(diffs):
        cb = diff.shape[1]
        ob = _dot(diff.astype(BF16), wp_ref[g]) * ps_ref[:, g * cb:(g + 1) * cb]
        y = y + _dot(ob.astype(BF16), wo_ref[da + g * cb:da + (g + 1) * cb, :])
    return x + y


def _even_out_prompt_kernel(x_ref, oa_ref, u_ref, halo_ref, wp_ref, ps_ref, wo_ref, o_ref, pad_scr, *,
                            tp, tiles_per_seq, cb):
    it = pl.program_id(0) % tiles_per_seq
    pad_scr[0:POOL_HALO, :] = jnp.where(it == 0, 0.0, halo_ref[...])
    pad_scr[POOL_HALO:POOL_HALO + tp, :] = u_ref[...]
    pos = it * tp + _iota2((tp, 1), 0)
    diffs = []
    for g, w in enumerate(POOL_WINDOWS):
        cols = slice(g * cb, (g + 1) * cb)
        s = pad_scr[POOL_HALO:POOL_HALO + tp, cols]
        for j in range(1, w):
            s = s + pad_scr[POOL_HALO - j:POOL_HALO - j + tp, cols]
        cnt = jnp.minimum(w, pos + 1).astype(F32)
        diffs.append(s / cnt - u_ref[:, cols])
    o_ref[...] = _even_out_tail(x_ref[...], oa_ref[...], diffs, wp_ref, ps_ref, wo_ref)


def _even_out_prompt(x, oa, u, w_pool, pool_scale, w_out, layer, seq):
    n, d = x.shape
    db_ = u.shape[1]
    tp = _tile(seq, 512)
    tiles_per_seq = seq // tp
    hb = tp // POOL_HALO
    g_b = w_pool.shape[1]
    cb = db_ // g_b
    return pl.pallas_call(
        functools.partial(_even_out_prompt_kernel, tp=tp, tiles_per_seq=tiles_per_seq, cb=cb),
        grid=(n // tp,),
        in_specs=[
            pl.BlockSpec((tp, d), lambda i: (i, 0)),
            pl.BlockSpec((tp, oa.shape[1]), lambda i: (i, 0)),
            pl.BlockSpec((tp, db_), lambda i: (i, 0)),
            pl.BlockSpec((POOL_HALO, db_), lambda i: (jnp.maximum(i * hb - 1, 0), 0)),
            _resident((None, g_b, cb, cb), lambda i: (layer, 0, 0, 0)),
            pl.BlockSpec((None, 1, db_), lambda i: (layer, 0, 0)),
            _resident((None, d, d), lambda i: (layer, 0, 0)),
        ],
        out_specs=pl.BlockSpec((tp, d), lambda i: (i, 0)),
        out_shape=jax.ShapeDtypeStruct((n, d), F32),
        scratch_shapes=[pltpu.VMEM((POOL_HALO + tp, db_), F32)],
        compiler_params=_cparams("parallel"),
        name="even_out_prompt",
    )(x, oa, u, u, w_pool, pool_scale, w_out)


def _even_out_sample_kernel(x_ref, oa_ref, p_ref, wp_ref, ps_ref, wo_ref, o_ref, *, bb, t_new, pos0, cb):
    pos = pos0 + _iota2((1, t_new, 1), 1)
    diffs = []
    for g, w in enumerate(POOL_WINDOWS):
        cols = slice(g * cb, (g + 1) * cb)
        u3 = p_ref[:, POOL_HALO:POOL_HALO + t_new, cols]
        s = u3
        for j in range(1, w):
            s = s + p_ref[:, POOL_HALO - j:POOL_HALO - j + t_new, cols]
        cnt = jnp.minimum(w, pos + 1).astype(F32)
        diffs.append((s / cnt - u3).reshape(bb * t_new, cb))
    o_ref[...] = _even_out_tail(x_ref[...], oa_ref[...], diffs, wp_ref, ps_ref, wo_ref)


def _even_out_sample(x, oa, padded, w_pool, pool_scale, w_out, layer, t_new, pos0):
    n, d = x.shape
    nb, rows, db_ = padded.shape
    bb = max(b for b in range(1, min(nb, 64) + 1) if nb % b == 0)
    g_b = w_pool.shape[1]
    cb = db_ // g_b
    return pl.pallas_call(
        functools.partial(_even_out_sample_kernel, bb=bb, t_new=t_new, pos0=pos0, cb=cb),
        grid=(nb // bb,),
        in_specs=[
            pl.BlockSpec((bb * t_new, d), lambda i: (i, 0)),
            pl.BlockSpec((bb * t_new, oa.shape[1]), lambda i: (i, 0)),
            pl.BlockSpec((bb, rows, db_), lambda i: (i, 0, 0)),
            _resident((None, g_b, cb, cb), lambda i: (layer, 0, 0, 0)),
            pl.BlockSpec((None, 1, db_), lambda i: (layer, 0, 0)),
            _resident((None, d, d), lambda i: (layer, 0, 0)),
        ],
        out_specs=pl.BlockSpec((bb * t_new, d), lambda i: (i, 0)),
        out_shape=jax.ShapeDtypeStruct((n, d), F32),
        compiler_params=_cparams("parallel"),
        name="even_out_sample",
    )(x, oa, padded, w_pool, pool_scale, w_out)


CONV_PAD = SUBLANES


def _bdot(a, b):
    return lax.dot_general(a, b, (((2,), (1,)), ((0,), (0,))), preferred_element_type=F32)


def _bdot_nt(a, b):
    return lax.dot_general(a, b, (((2,), (2,)), ((0,), (0,))), preferred_element_type=F32)


def _bdot_tn(a, b):
    return lax.dot_general(a, b, (((1,), (1,)), ((0,), (0,))), preferred_element_type=F32)


def _unit_lower_inverse(m, eye):
    n = m.shape[-1]
    p = eye - m
    x = m
    k = 2
    while k < n:
        xb = x.astype(BF16)
        x = _bdot(xb, xb)
        p = p + _bdot(p.astype(BF16), x.astype(BF16))
        k *= 2
    return p


def _gdn_kernel(alog_ref, dtb_ref, qkv_ref, z_ref, ab_ref, cbuf_ref, s0_ref, cw_ref, gn_ref,
                o_ref, sfin_ref, xpad, q_scr, k_scr, v_scr, gc_scr, df_scr, be_scr, s_scr, *,
                bb, tb, chunk, heads, dk, dv, cw):
    t = pl.program_id(1)
    tail = cw - 1
    rows_all = bb * tb
    nb = bb * heads

    @pl.when(t == 0)
    def _():
        s_scr[...] = s0_ref[...]
        xpad[:, CONV_PAD - tail:CONV_PAD, :] = cbuf_ref[...]

    xpad[:, CONV_PAD:CONV_PAD + tb, :] = qkv_ref[...]

    def conv_silu(c0, width):
        acc = None
        for i in range(cw):
            r0 = CONV_PAD - tail + i
            term = xpad[:, r0:r0 + tb, c0:c0 + width] * cw_ref[i:i + 1, c0:c0 + width]
            acc = term if acc is None else acc + term
        return _silu(acc)

    def l2n(x):
        return x * lax.rsqrt(jnp.sum(x * x, axis=-1, keepdims=True) + EPS)

    for h in range(heads):
        q_scr[:, h] = l2n(conv_silu(h * dk, dk)) * (dk ** -0.5)
        k_scr[:, h] = l2n(conv_silu(heads * dk + h * dk, dk))
        v_scr[:, h] = conv_silu(2 * heads * dk + h * dv, dv)

    xpad[:, CONV_PAD - tail:CONV_PAD, :] = xpad[:, CONV_PAD + tb - tail:CONV_PAD + tb, :]

    ab = ab_ref[...].reshape(rows_all, LANES)
    g = -jnp.exp(alog_ref[...]) * _softplus(ab + dtb_ref[...])
    beta = jax.nn.sigmoid(ab)
    rr = _iota2((rows_all, rows_all), 0)
    cc = _iota2((rows_all, rows_all), 1)
    l_blk = jnp.where((rr // chunk == cc // chunk) & (rr >= cc), 1.0, 0.0).astype(BF16)
    after = (_iota2((rows_all, LANES), 0) % chunk) > _iota2((rows_all, LANES), 1)
    for h in range(heads):
        g_rep = jnp.broadcast_to(g[:, h:h + 1], (rows_all, LANES))
        rhs = jnp.concatenate([g_rep, jnp.where(after, g_rep, 0.0)], axis=1)
        hi = rhs.astype(BF16)
        lo = (rhs - hi.astype(F32)).astype(BF16)
        res = _dot(l_blk, hi) + _dot(l_blk, lo)
        gc_scr[:, h] = res[:, 0:LANES].reshape(bb, tb, LANES)
        df_scr[:, h] = res[:, LANES:2 * LANES].reshape(bb, tb, LANES)
        be_scr[:, h] = jnp.broadcast_to(beta[:, heads + h:heads + h + 1], (rows_all, LANES)).reshape(
            bb, tb, LANES)

    row = _iota2((chunk, chunk), 0)
    col = _iota2((chunk, chunk), 1)
    lower = row >= col
    strict = row > col
    eye = jnp.where(row == col, 1.0, 0.0)

    def do_chunk(c, _):
        r0 = pl.multiple_of(c * chunk, chunk)
        rows = pl.ds(r0, chunk)

        def ld(ref):
            return ref[:, :, rows, :].reshape(nb, chunk, LANES)

        q, k, v, gc, df, be = ld(q_scr), ld(k_scr), ld(v_scr), ld(gc_scr), ld(df_scr), ld(be_scr)
        kb = k * be
        decay = jnp.where(lower, jnp.exp(jnp.where(lower, df[:, :, 0:chunk], 0.0)), 0.0)
        kq = _bdot_nt(jnp.concatenate([kb, q], axis=1).astype(BF16), k.astype(BF16))
        m = jnp.where(strict, kq[:, 0:chunk] * decay, 0.0)
        attn = kq[:, chunk:2 * chunk] * decay
        t_inv = _unit_lower_inverse(m, eye)
        eg = jnp.exp(gc)
        sol = _bdot(t_inv.astype(BF16), jnp.concatenate([v * be, kb * eg], axis=2).astype(BF16))
        u_c = sol[:, :, 0:dv]
        w_c = sol[:, :, dv:dv + dk]
        gc_last = gc[:, chunk - 1:chunk, :]
        k_tail = k * jnp.exp(gc_last - gc)
        s = s_scr[...].reshape(nb, dk, dv)
        ws = _bdot(jnp.concatenate([w_c, q * eg], axis=1).astype(BF16), s.astype(BF16))
        v_new = u_c - ws[:, 0:chunk]
        v_new_b = v_new.astype(BF16)
        o = ws[:, chunk:2 * chunk] + _bdot(attn.astype(BF16), v_new_b)
        s_new = s * jnp.exp(gc_last) + _bdot_tn(k_tail.astype(BF16), v_new_b)
        s_scr[...] = s_new.reshape(bb, heads, dk, dv)
        zg = jnp.stack([z_ref[:, rows, h * dv:(h + 1) * dv] for h in range(heads)], axis=1)
        og = (_rms(o, gn_ref[...]) * _silu(zg.reshape(nb, chunk, dv))).reshape(bb, heads, chunk, dv)
        for h in range(heads):
            o_ref[:, rows, h * dv:(h + 1) * dv] = og[:, h]
        return 0

    lax.fori_loop(0, tb // chunk, do_chunk, 0)

    @pl.when(t == pl.num_programs(1) - 1)
    def _():
        sfin_ref[...] = s_scr[...]


def _gdn(qkv, z, ab, conv_buf, s0, conv_w, a_log_row, dt_bias_row, gn_w, layer, batch, seq, chunk_pref,
         bb):
    n, dconv = qkv.shape
    _, heads, dk, dv = s0.shape
    assert dk == LANES and dv == LANES
    cw = conv_w.shape[1]
    tb = _tile(seq, 256)
    chunk = _tile(tb, chunk_pref)
    nt = seq // tb
    dz = heads * dv
    assert batch % bb == 0
    row_map = lambda b, t: (b, t, 0)
    state_map = lambda b, t: (b, 0, 0, 0)
    og, s_new = pl.pallas_call(
        functools.partial(_gdn_kernel, bb=bb, tb=tb, chunk=chunk, heads=heads, dk=dk, dv=dv, cw=cw),
        grid=(batch // bb, nt),
        in_specs=[
            pl.BlockSpec((None, 1, LANES), lambda b, t: (layer, 0, 0)),
            pl.BlockSpec((None, 1, LANES), lambda b, t: (layer, 0, 0)),
            pl.BlockSpec((bb, tb, dconv), row_map),
            pl.BlockSpec((bb, tb, dz), row_map),
            pl.BlockSpec((bb, tb, LANES), row_map),
            pl.BlockSpec((bb, cw - 1, dconv), lambda b, t: (b, 0, 0)),
            pl.BlockSpec((bb, heads, dk, dv), state_map),
            pl.BlockSpec((None, cw, dconv), lambda b, t: (layer, 0, 0)),
            pl.BlockSpec((None, 1, dv), lambda b, t: (layer, 0, 0)),
        ],
        out_specs=[
            pl.BlockSpec((bb, tb, dz), row_map),
            pl.BlockSpec((bb, heads, dk, dv), state_map),
        ],
        out_shape=[
            jax.ShapeDtypeStruct((batch, seq, dz), F32),
            jax.ShapeDtypeStruct((batch, heads, dk, dv), F32),
        ],
        scratch_shapes=[pltpu.VMEM((bb, CONV_PAD + tb, dconv), F32)]
        + [pltpu.VMEM((bb, heads, tb, LANES), F32)] * 6
        + [pltpu.VMEM((bb, heads, dk, dv), F32)],
        compiler_params=_cparams("parallel", "arbitrary"),
        name="gdn",
    )(a_log_row, dt_bias_row, qkv.reshape(batch, seq, dconv), z.reshape(batch, seq, dz),
      ab.reshape(batch, seq, LANES), conv_buf, s0, conv_w, gn_w)
    return og.reshape(n, dz), s_new


def _out_proj_kernel(x_ref, y_ref, w_ref, o_ref):
    o_ref[...] = x_ref[...] + _dot(y_ref[...].astype(BF16), w_ref[...])


def _out_proj(x, y, w, layer):
    n, d = x.shape
    tm = _tile(n, 512)
    return pl.pallas_call(
        _out_proj_kernel,
        grid=(n // tm,),
        in_specs=[
            pl.BlockSpec((tm, d), lambda i: (i, 0)),
            pl.BlockSpec((tm, y.shape[1]), lambda i: (i, 0)),
            _resident((None, y.shape[1], d), lambda i: (layer, 0, 0)),
        ],
        out_specs=pl.BlockSpec((tm, d), lambda i: (i, 0)),
        out_shape=jax.ShapeDtypeStruct((n, d), F32),
        compiler_params=_cparams("parallel"),
        name="out_proj",
    )(x, y, w)


def _trunk(x, batch, seq, pos0, cache, pool_bufs, conv_bufs, rec_states, w):
    depth = w["norm_mix"].shape[0]
    heads_a, dh = w["heads_a"], w["dh"]
    da = heads_a * dh
    new_k, new_v, new_pool, new_conv, new_rec = [], [], [], [], []
    for layer in range(depth):
        i = layer // 2
        x = _ffn(x, w["norm_ffn"], w["w_ffn_in"], w["w_ffn_out"], layer, 0)
        if layer % 2 == 0:
            db_ = w["w_in_even"].shape[2] - 3 * da
            segs = [(0, da, (0,)), (da, 2 * da, (1,)), (2 * da, 3 * da, (2,)), (3 * da, 3 * da + db_, (3,))]
            dts = [F32, F32, F32, F32]
            wds = [da, da, da, db_]
            if cache is None:
                segs[1] = (da, 2 * da, (1, 4))
                segs[2] = (2 * da, 3 * da, (2, 5))
                dts += [BF16, BF16]
                wds += [da, da]
            outs = _proj(x, (w["norm_mix"], layer), (w["w_in_even"], i), segs, dts, wds, "in_even")
            q, k, v, u = outs[:4]
            if cache is None:
                oa = _attn_prompt(q, outs[4], outs[5], w["sb_bias"][i][None], batch, seq, dh)
                x = _even_out_prompt(x, oa, u, w["w_pool"], w["pool_scale"], w["w_out_even"], i, seq)
                new_pool.append(u.reshape(batch, seq, db_)[:, seq - (POOL_HALO - 1):])
            else:
                cache_k, cache_v, page_table = cache
                oa = _attn_sample(q, k, v, cache_k, cache_v, i, page_table, w["sb_bias"][i][None], seq)
                u3 = u.reshape(batch, seq, db_)
                buf = pool_bufs[i]
                padded = jnp.concatenate(
                    [jnp.zeros((batch, POOL_HALO - buf.shape[1], db_), F32), buf, u3], axis=1)
                x = _even_out_sample(x, oa, padded, w["w_pool"], w["pool_scale"], w["w_out_even"], i,
                                     seq, pos0)
                new_pool.append(padded[:, padded.shape[1] - buf.shape[1]:])
            new_k.append(k.reshape(batch, seq, heads_a, dh))
            new_v.append(v.reshape(batch, seq, heads_a, dh))
        else:
            heads_c, dk, dv = rec_states.shape[2:]
            dconv = heads_c * (2 * dk + dv)
            dz = heads_c * dv
            segs = [(0, dconv, (0,)), (dconv, dconv + dz, (1,)), (dconv + dz, dconv + dz + LANES, (2,))]
            qkv, z, ab = _proj(x, (w["norm_mix"], layer), (w["w_in_odd"], i), segs, [F32] * 3,
                               [dconv, dz, LANES], "in_odd")
            og, s_new = _gdn(qkv, z, ab, conv_bufs[i], rec_states[i], w["conv_w"], w["a_log"],
                             w["dt_bias"], w["gn_w"], i, batch, seq, 64, 1 if cache is None else 4)
            x = _out_proj(x, og, w["w_out_odd"], i)
            tail = conv_bufs.shape[2]
            xp = jnp.concatenate([conv_bufs[i], qkv.reshape(batch, seq, dconv)], axis=1)
            new_conv.append(xp[:, xp.shape[1] - tail:])
            new_rec.append(s_new)
        final_g = w["norm_final"] if layer == depth - 1 else None
        x = _ffn(x, w["norm_ffn"], w["w_ffn_in"], w["w_ffn_out"], layer, 1, final_g)
    return x, jnp.stack(new_k), jnp.stack(new_v), jnp.stack(new_pool), jnp.stack(new_conv), jnp.stack(new_rec)


def kernel(x_prompt, x_sample, cache_k, cache_v, page_table, state_pool, state_conv, state_rec, norm_ffn,
           w_ffn_in, w_ffn_out, norm_mix, w_in_even, sb_bias, w_pool, pool_scale, w_out_even, w_in_odd,
           conv_w, a_log, dt_bias, gn_w, w_out_odd, norm_final):
    bp, seq, d = x_prompt.shape
    db, dseq, _ = x_sample.shape
    n_even, n_pool, page, heads_a, dh = cache_k.shape
    n_odd, _, heads_c, dk, dv = state_rec.shape
    da = heads_a * dh
    dconv = heads_c * (2 * dk + dv)
    dz = heads_c * dv
    depth = norm_mix.shape[0]

    def lane_row(p):
        return jnp.pad(p.astype(F32), ((0, 0), (0, LANES - p.shape[1])))[:, None, :]

    w_odd = jnp.concatenate(
        [w_in_odd, jnp.zeros((n_odd, d, LANES - 2 * heads_c), w_in_odd.dtype)], axis=2).astype(BF16)
    w = dict(
        heads_a=heads_a, dh=dh,
        norm_ffn=norm_ffn.reshape(depth, 2, 1, d),
        w_ffn_in=w_ffn_in.astype(BF16),
        w_ffn_out=w_ffn_out.astype(BF16),
        norm_mix=norm_mix.reshape(depth, 1, d),
        w_in_even=w_in_even.astype(BF16),
        sb_bias=sb_bias.astype(F32),
        w_pool=w_pool.astype(BF16),
        pool_scale=pool_scale.reshape(n_even, 1, -1),
        w_out_even=w_out_even.astype(BF16),
        w_in_odd=w_odd,
        conv_w=conv_w,
        a_log=lane_row(a_log),
        dt_bias=lane_row(dt_bias),
        gn_w=gn_w.reshape(n_odd, 1, dv),
        w_out_odd=w_out_odd.astype(BF16),
        norm_final=norm_final.reshape(1, d),
    )
    past_len = page_table.shape[1] * page

    yp, kp, vp, poolp, convp, recp = _trunk(
        x_prompt.reshape(bp * seq, d), bp, seq, 0, None,
        None,
        jnp.zeros((n_odd, bp, conv_w.shape[1] - 1, dconv), state_conv.dtype),
        jnp.zeros((n_odd, bp, heads_c, dk, dv), state_rec.dtype), w)
    ys, ks, vs, pools, convs, recs = _trunk(
        x_sample.reshape(db * dseq, d), db, dseq, past_len, (cache_k, cache_v, page_table),
        state_pool, state_conv, state_rec, w)
    return (yp.reshape(bp, seq, d), ys.reshape(db, dseq, d), kp, vp, ks, vs, poolp, pools, convp, convs,
            recp, recs)
```

```python
import functools

import jax
import jax.numpy as jnp
from jax import lax
from jax.experimental import pallas as pl
from jax.experimental.pallas import tpu as pltpu

F32 = jnp.float32
BF16 = jnp.bfloat16
EPS = 1e-6
FFN_HALF = 0.5
POOL_WINDOWS = (2, 4, 8, 16)
POOL_HALO = 16
LANES = 128
SUBLANES = 8
VMEM_LIMIT = 56 * 1024 * 1024
NEG_LOG2E = -1.4426950408889634


def _cparams(*sem):
    return pltpu.CompilerParams(dimension_semantics=sem, vmem_limit_bytes=VMEM_LIMIT)


def _tile(n, pref):
    t = pref
    while t > SUBLANES and n % t:
        t //= 2
    assert n % t == 0, (n, pref)
    return t


def _dot(a, b):
    return jnp.dot(a, b, preferred_element_type=F32)


def _dot_nt(a, b):
    return lax.dot_general(a, b, (((1,), (1,)), ((), ())), preferred_element_type=F32)


def _rms(x, g):
    return x * lax.rsqrt(jnp.mean(x * x, axis=-1, keepdims=True) + EPS) * g


def _softplus(x):
    return jnp.maximum(x, 0.0) + jnp.log(1.0 + jnp.exp2(jnp.abs(x) * NEG_LOG2E))


def _silu(x):
    return x * jax.nn.sigmoid(x)


def _iota2(shape, dim):
    return lax.broadcasted_iota(jnp.int32, shape, dim)


def _resident(shape, index_map):
    return pl.BlockSpec(shape, index_map, pipeline_mode=pl.Buffered(1))


def _ffn_kernel(*refs, n_pre, n_scratch, pre_fn, d_ff, ck, final):
    pre_refs = refs[:n_pre]
    scratch = refs[len(refs) - n_scratch:]
    g_ref, wi_ref, wo_ref, *rest = refs[n_pre:len(refs) - n_scratch]
    o_ref = rest[-1]
    x = pre_fn(*pre_refs, *scratch)
    h = _rms(x, g_ref[...]).astype(BF16)
    acc = None
    for c in range(d_ff // ck):
        gate = _dot(h, wi_ref[:, c * ck:(c + 1) * ck])
        up = _dot(h, wi_ref[:, d_ff + c * ck:d_ff + (c + 1) * ck])
        act = (_silu(gate) * up).astype(BF16)
        part = _dot(act, wo_ref[c * ck:(c + 1) * ck, :])
        acc = part if acc is None else acc + part
    y = x + FFN_HALF * acc
    if final:
        y = _rms(y, rest[0][...])
    o_ref[...] = y


FFN_ROWS = 512


def _ffn(pre, n, d, norm_ffn, w_in, w_out, layer, j, final_g=None):
    pre_args, pre_specs, pre_fn, scratch_shapes = pre
    d_ff = w_out.shape[2]
    tm = _tile(n, FFN_ROWS)
    ck = 256 if d_ff % 256 == 0 else d_ff
    in_specs = list(pre_specs) + [
        pl.BlockSpec((None, None, 1, d), lambda i: (layer, j, 0, 0)),
        _resident((None, None, d, 2 * d_ff), lambda i: (layer, j, 0, 0)),
        _resident((None, None, d_ff, d), lambda i: (layer, j, 0, 0)),
    ]
    args = list(pre_args) + [norm_ffn, w_in, w_out]
    if final_g is not None:
        in_specs.append(pl.BlockSpec((1, d), lambda i: (0, 0)))
        args.append(final_g)
    return pl.pallas_call(
        functools.partial(_ffn_kernel, n_pre=len(pre_args), n_scratch=len(scratch_shapes), pre_fn=pre_fn,
                          d_ff=d_ff, ck=ck, final=final_g is not None),
        grid=(n // tm,),
        in_specs=in_specs,
        out_specs=pl.BlockSpec((tm, d), lambda i: (i, 0)),
        out_shape=jax.ShapeDtypeStruct((n, d), F32),
        scratch_shapes=list(scratch_shapes),
        compiler_params=_cparams("parallel"),
        name="ffn",
    )(*args)


def _pre_plain(x):
    n, d = x.shape
    tm = _tile(n, FFN_ROWS)
    return ([x], [pl.BlockSpec((tm, d), lambda i: (i, 0))], lambda x_ref: x_ref[...], [])


def _proj_kernel(x_ref, g_ref, w_ref, *o_refs, segs):
    h = _rms(x_ref[...], g_ref[...]).astype(BF16)
    for (a, b, outs) in segs:
        y = _dot(h, w_ref[:, a:b])
        for oi in outs:
            o_refs[oi][...] = y.astype(o_refs[oi].dtype)


def _proj(x, g, w, segs, out_dtypes, out_widths, name):
    n, d = x.shape
    tm = _tile(n, 512)
    g_arr, g_idx = g
    w_arr, w_idx = w
    in_specs = [
        pl.BlockSpec((tm, d), lambda i: (i, 0)),
        pl.BlockSpec((None, 1, d), lambda i: (g_idx, 0, 0)),
        _resident((None, d, w_arr.shape[2]), lambda i: (w_idx, 0, 0)),
    ]
    return pl.pallas_call(
        functools.partial(_proj_kernel, segs=segs),
        grid=(n // tm,),
        in_specs=in_specs,
        out_specs=[pl.BlockSpec((tm, wd), lambda i: (i, 0)) for wd in out_widths],
        out_shape=[jax.ShapeDtypeStruct((n, wd), dt) for wd, dt in zip(out_widths, out_dtypes)],
        compiler_params=_cparams("parallel"),
        name=name,
    )(x, g_arr, w_arr)


def _proj_even_prompt_kernel(x_ref, g_ref, w_ref, wkv_t_ref, q_ref, u_ref, kt_ref, vt_ref, ktb_ref, vtb_ref,
                             *, da):
    h = _rms(x_ref[...], g_ref[...]).astype(BF16)
    q_ref[...] = _dot(h, w_ref[:, 0:da])
    u_ref[...] = _dot(h, w_ref[:, 3 * da:])
    kv_t = _dot_nt(wkv_t_ref[...], h)
    kt_ref[...] = kv_t[0:da]
    vt_ref[...] = kv_t[da:2 * da]
    ktb_ref[...] = kv_t[0:da].astype(BF16)
    vtb_ref[...] = kv_t[da:2 * da].astype(BF16)


def _proj_even_prompt(x, norm_mix, layer, w, wkv_t, i, batch, seq, da):
    n, d = x.shape
    tm = _tile(seq, 512)
    per_seq = seq // tm
    db_ = w.shape[2] - 3 * da
    row_spec = lambda wd: pl.BlockSpec((tm, wd), lambda t: (t, 0))
    lane_spec = pl.BlockSpec((None, da, tm), lambda t: (t // per_seq, 0, t % per_seq))
    return pl.pallas_call(
        functools.partial(_proj_even_prompt_kernel, da=da),
        grid=(n // tm,),
        in_specs=[
            row_spec(d),
            pl.BlockSpec((None, 1, d), lambda t: (layer, 0, 0)),
            _resident((None, d, w.shape[2]), lambda t: (i, 0, 0)),
            _resident((None, 2 * da, d), lambda t: (i, 0, 0)),
        ],
        out_specs=[row_spec(da), row_spec(db_), lane_spec, lane_spec, lane_spec, lane_spec],
        out_shape=[
            jax.ShapeDtypeStruct((n, da), F32),
            jax.ShapeDtypeStruct((n, db_), F32),
            jax.ShapeDtypeStruct((batch, da, seq), F32),
            jax.ShapeDtypeStruct((batch, da, seq), F32),
            jax.ShapeDtypeStruct((batch, da, seq), BF16),
            jax.ShapeDtypeStruct((batch, da, seq), BF16),
        ],
        compiler_params=_cparams("parallel"),
        name="in_even_prompt",
    )(x, norm_mix, w, wkv_t)


def _suffix_matrix(n):
    row = _iota2((2 * n, n), 0) % n
    col = _iota2((2 * n, n), 1)
    return jnp.where(row > col, 1.0, 0.0).astype(BF16)


ATTN_LANE_BLOCKS = 2


def _attn_prompt_kernel(bias_ref, q_ref, k_ref, v_ref, o_ref, r_scr, acc_scr, *, tq, dh, scale, groups):
    hg = pl.program_id(1)
    i = pl.program_id(2)
    nh = LANES // dh
    lane = _iota2((tq, LANES), 1)
    heads = [(lane >= e * dh) & (lane < (e + 1) * dh) for e in range(nh)]
    u_tri = _suffix_matrix(tq)
    causal = _iota2((nh * tq, tq), 1) < (_iota2((nh * tq, tq), 0) % tq)
    qs, biases = [], []
    for g in range(groups):
        q = q_ref[:, g * LANES:(g + 1) * LANES] * scale
        qs.append(jnp.concatenate([jnp.where(hm, q, 0.0) for hm in heads], axis=0).astype(BF16))
        biases.append([bias_ref[0, (hg * groups + g) * nh + e] for e in range(nh)])

    def tile(j, valid):
        r0 = pl.multiple_of(j * tq, tq)
        gs = range(groups)
        zs = [_dot(qs[g], k_ref[g * LANES:(g + 1) * LANES, pl.ds(r0, tq)]) for g in gs]
        zs = [jnp.concatenate([zs[g][e * tq:(e + 1) * tq] + biases[g][e] for e in range(nh)], axis=0)
              for g in gs]
        sps = [_softplus(z) for z in zs]
        logsig = [z - sp for z, sp in zip(zs, sps)]
        drops = sps if valid is None else [jnp.where(valid, sp, 0.0) for sp in sps]
        his = [d.astype(BF16) for d in drops]
        los = [(d - hi.astype(F32)).astype(BF16) for d, hi in zip(drops, his)]
        laters = [_dot(jnp.concatenate([hi, lo], axis=1), u_tri) for hi, lo in zip(his, los)]
        for g in gs:
            a = jnp.exp(logsig[g] - laters[g] - r_scr[g])
            if valid is not None:
                a = jnp.where(valid, a, 0.0)
            acc_scr[g] += _dot_nt(a.astype(BF16), v_ref[g * LANES:(g + 1) * LANES, pl.ds(r0, tq)])
            r_scr[g] += jnp.sum(drops[g], axis=1, keepdims=True)

    r_scr[...] = jnp.zeros_like(r_scr)
    acc_scr[...] = jnp.zeros_like(acc_scr)
    tile(i, causal)

    def body(s, c):
        tile(i - 1 - s, None)
        return c

    lax.fori_loop(0, i, body, 0)
    for g in range(groups):
        out = jnp.where(heads[0], acc_scr[g, 0:tq], 0.0)
        for e in range(1, nh):
            out = out + jnp.where(heads[e], acc_scr[g, e * tq:(e + 1) * tq], 0.0)
        o_ref[:, g * LANES:(g + 1) * LANES] = out


def _attn_prompt(q, k_bf, v_bf, sb_bias, batch, seq, dh):
    n, da = q.shape
    tq = _tile(seq, 256)
    nq = seq // tq
    groups = ATTN_LANE_BLOCKS if da % (ATTN_LANE_BLOCKS * LANES) == 0 else 1
    wl = groups * LANES
    rows = LANES // dh * tq
    return pl.pallas_call(
        functools.partial(_attn_prompt_kernel, tq=tq, dh=dh, scale=dh ** -0.5, groups=groups),
        grid=(batch, da // wl, nq),
        in_specs=[
            pl.BlockSpec(memory_space=pltpu.SMEM),
            pl.BlockSpec((tq, wl), lambda b, hg, i: (b * nq + i, hg)),
            pl.BlockSpec((None, wl, seq), lambda b, hg, i: (b, hg, 0)),
            pl.BlockSpec((None, wl, seq), lambda b, hg, i: (b, hg, 0)),
        ],
        out_specs=pl.BlockSpec((tq, wl), lambda b, hg, i: (b * nq + i, hg)),
        out_shape=jax.ShapeDtypeStruct((n, da), F32),
        scratch_shapes=[pltpu.VMEM((groups, rows, 1), F32), pltpu.VMEM((groups, rows, LANES), F32)],
        compiler_params=_cparams("parallel", "parallel", "arbitrary"),
        name="attn_prompt",
    )(sb_bias, q, k_bf, v_bf)


def _attn_sample_kernel(pt_ref, bias_ref, q_ref, kn_ref, vn_ref, *rest, n_pages, page, t_new, heads, dh,
                        scale):
    del pt_ref
    k_pages = rest[:n_pages]
    v_pages = rest[n_pages:2 * n_pages]
    o_ref, r_scr, acc_scr = rest[2 * n_pages:]
    j = pl.program_id(1)
    ht = heads * t_new
    q = q_ref[...] * scale
    qh = [q[:, h * dh:(h + 1) * dh].astype(BF16) for h in range(heads)]
    u_tri = _suffix_matrix(page)

    def tile(logits, weighted, valid, slots):
        z = jnp.concatenate([logits(h) + bias_ref[0, h] for h in range(heads)], axis=0)
        sp = _softplus(z)
        drop = sp if valid is None else jnp.where(valid, sp, 0.0)
        hi = drop.astype(BF16)
        lo = (drop - hi.astype(F32)).astype(BF16)
        cols = [slice(p * page, (p + 1) * page) for p in range(slots)]
        later = _dot(jnp.concatenate([jnp.concatenate([hi[:, c], lo[:, c]], axis=1) for c in cols], axis=0),
                     u_tri)
        r_run = r_scr[...]
        shift = []
        for p, c in enumerate(cols):
            shift.append(later[p * ht:(p + 1) * ht] + r_run)
            r_run = r_run + jnp.sum(drop[:, c], axis=1, keepdims=True)
        r_scr[...] = r_run
        a = jnp.exp(z - sp - jnp.concatenate(shift, axis=1))
        if valid is not None:
            a = jnp.where(valid, a, 0.0)
        acc_scr[...] += jnp.concatenate(
            [weighted(h, a[h * t_new:(h + 1) * t_new].astype(BF16)) for h in range(heads)], axis=0)

    @pl.when(j == 0)
    def _():
        r_scr[...] = jnp.zeros_like(r_scr)
        acc_scr[...] = jnp.zeros_like(acc_scr)
        pad = jnp.zeros((page - t_new, dh), F32)
        kn = kn_ref[...]
        vn = vn_ref[...]
        valid_new = _iota2((ht, page), 1) < (_iota2((ht, page), 0) % t_new)

        def padded(x, h):
            return jnp.concatenate([x[:, h * dh:(h + 1) * dh], pad], axis=0).astype(BF16)

        tile(lambda h: _dot_nt(qh[h], padded(kn, h)), lambda h, a: _dot(a, padded(vn, h)), valid_new, 1)

    def lanes_of(refs, h):
        return jnp.concatenate([r[h] for r in refs], axis=1).astype(BF16)

    tile(lambda h: _dot(qh[h], lanes_of(k_pages, h)), lambda h, a: _dot_nt(a, lanes_of(v_pages, h)),
         None, n_pages)

    @pl.when(j == pl.num_programs(1) - 1)
    def _():
        o_ref[...] = jnp.concatenate(
            [acc_scr[h * t_new:(h + 1) * t_new, :] for h in range(heads)], axis=1)


SAMPLE_PAGES_PER_STEP = 16


def _attn_sample(q, k_new, v_new, cache_k, cache_v, layer, page_table, sb_bias, t_new):
    n, da = q.shape
    db, n_pages = page_table.shape
    _, _, page, heads, dh = cache_k.shape
    cache_k = jnp.transpose(cache_k, (0, 1, 3, 4, 2))
    cache_v = jnp.transpose(cache_v, (0, 1, 3, 4, 2))
    pps = SAMPLE_PAGES_PER_STEP if n_pages % SAMPLE_PAGES_PER_STEP == 0 else n_pages
    steps = n_pages // pps
    page_specs = [
        pl.BlockSpec((None, None, heads, dh, page),
                     lambda b, j, pt, p=p: (layer, pt[b, n_pages - 1 - (j * pps + p)], 0, 0, 0))
        for p in range(pps)
    ]
    row_spec = pl.BlockSpec((t_new, da), lambda b, j, pt: (b, 0))
    grid_spec = pltpu.PrefetchScalarGridSpec(
        num_scalar_prefetch=1,
        grid=(db, steps),
        in_specs=[pl.BlockSpec(memory_space=pltpu.SMEM), row_spec, row_spec, row_spec]
        + page_specs + page_specs,
        out_specs=row_spec,
        scratch_shapes=[pltpu.VMEM((heads * t_new, 1), F32), pltpu.VMEM((heads * t_new, dh), F32)],
    )
    return pl.pallas_call(
        functools.partial(_attn_sample_kernel, n_pages=pps, page=page, t_new=t_new, heads=heads,
                          dh=dh, scale=dh ** -0.5),
        grid_spec=grid_spec,
        out_shape=jax.ShapeDtypeStruct((n, da), F32),
        compiler_params=_cparams("parallel", "arbitrary"),
        name="attn_sample",
    )(page_table, sb_bias, q, k_new, v_new, *([cache_k] * pps), *([cache_v] * pps))


def _even_out_tail(x, oa, diffs, wp_ref, ps_ref, wo_ref):
    mixed = [oa.astype(BF16)]
    for g, diff in enumerate(diffs):
        cb = diff.shape[1]
        ob = _dot(diff.astype(BF16), wp_ref[g]) * ps_ref[:, g * cb:(g + 1) * cb]
        mixed.append(ob.astype(BF16))
    return x + _dot(jnp.concatenate(mixed, axis=1), wo_ref[...])


def _even_out_prompt_stage(x_ref, oa_ref, u_ref, halo_ref, wp_ref, ps_ref, wo_ref, pad_scr, *,
                           tp, tiles_per_seq, cb):
    it = pl.program_id(0) % tiles_per_seq
    pad_scr[0:POOL_HALO, :] = jnp.where(it == 0, 0.0, halo_ref[...])
    pad_scr[POOL_HALO:POOL_HALO + tp, :] = u_ref[...]
    pos = it * tp + _iota2((tp, 1), 0)
    diffs = []
    for g, w in enumerate(POOL_WINDOWS):
        cols = slice(g * cb, (g + 1) * cb)
        s = pad_scr[POOL_HALO:POOL_HALO + tp, cols]
        for j in range(1, w):
            s = s + pad_scr[POOL_HALO - j:POOL_HALO - j + tp, cols]
        cnt = jnp.minimum(w, pos + 1).astype(F32)
        diffs.append(s / cnt - u_ref[:, cols])
    return _even_out_tail(x_ref[...], oa_ref[...], diffs, wp_ref, ps_ref, wo_ref)


def _pre_even_prompt(x, oa, u, w_pool, pool_scale, w_out, layer, seq):
    n, d = x.shape
    db_ = u.shape[1]
    tp = _tile(n, FFN_ROWS)
    assert seq % tp == 0 and tp % POOL_HALO == 0
    tiles_per_seq = seq // tp
    hb = tp // POOL_HALO
    g_b = w_pool.shape[1]
    cb = db_ // g_b
    specs = [
        pl.BlockSpec((tp, d), lambda i: (i, 0)),
        pl.BlockSpec((tp, oa.shape[1]), lambda i: (i, 0)),
        pl.BlockSpec((tp, db_), lambda i: (i, 0)),
        pl.BlockSpec((POOL_HALO, db_), lambda i: (jnp.maximum(i * hb - 1, 0), 0)),
        _resident((None, g_b, cb, cb), lambda i: (layer, 0, 0, 0)),
        pl.BlockSpec((None, 1, db_), lambda i: (layer, 0, 0)),
        _resident((None, d, d), lambda i: (layer, 0, 0)),
    ]
    fn = functools.partial(_even_out_prompt_stage, tp=tp, tiles_per_seq=tiles_per_seq, cb=cb)
    return ([x, oa, u, u, w_pool, pool_scale, w_out], specs, fn, [pltpu.VMEM((POOL_HALO + tp, db_), F32)])


def _even_out_sample_stage(x_ref, oa_ref, p_ref, wp_ref, ps_ref, wo_ref, *, bb, t_new, pos0, cb):
    pos = pos0 + _iota2((1, t_new, 1), 1)
    diffs = []
    for g, w in enumerate(POOL_WINDOWS):
        cols = slice(g * cb, (g + 1) * cb)
        u3 = p_ref[:, POOL_HALO:POOL_HALO + t_new, cols]
        s = u3
        for j in range(1, w):
            s = s + p_ref[:, POOL_HALO - j:POOL_HALO - j + t_new, cols]
        cnt = jnp.minimum(w, pos + 1).astype(F32)
        diffs.append((s / cnt - u3).reshape(bb * t_new, cb))
    return _even_out_tail(x_ref[...], oa_ref[...], diffs, wp_ref, ps_ref, wo_ref)


def _pre_even_sample(x, oa, padded, w_pool, pool_scale, w_out, layer, t_new, pos0):
    n, d = x.shape
    nb, rows, db_ = padded.shape
    tm = _tile(n, FFN_ROWS)
    assert tm % t_new == 0
    bb = tm // t_new
    g_b = w_pool.shape[1]
    cb = db_ // g_b
    specs = [
        pl.BlockSpec((tm, d), lambda i: (i, 0)),
        pl.BlockSpec((tm, oa.shape[1]), lambda i: (i, 0)),
        pl.BlockSpec((bb, rows, db_), lambda i: (i, 0, 0)),
        _resident((None, g_b, cb, cb), lambda i: (layer, 0, 0, 0)),
        pl.BlockSpec((None, 1, db_), lambda i: (layer, 0, 0)),
        _resident((None, d, d), lambda i: (layer, 0, 0)),
    ]
    fn = functools.partial(_even_out_sample_stage, bb=bb, t_new=t_new, pos0=pos0, cb=cb)
    return ([x, oa, padded, w_pool, pool_scale, w_out], specs, fn, [])


CONV_PAD = SUBLANES


def _bdot(a, b):
    return lax.dot_general(a, b, (((2,), (1,)), ((0,), (0,))), preferred_element_type=F32)


def _bdot_nt(a, b):
    return lax.dot_general(a, b, (((2,), (2,)), ((0,), (0,))), preferred_element_type=F32)


def _bdot_tn(a, b):
    return lax.dot_general(a, b, (((1,), (1,)), ((0,), (0,))), preferred_element_type=F32)


def _unit_lower_inverse(m, eye):
    n = m.shape[-1]
    p = eye - m
    x = m
    k = 2
    while k < n:
        xb = x.astype(BF16)
        x = _bdot(xb, xb)
        p = p + _bdot(p.astype(BF16), x.astype(BF16))
        k *= 2
    return p


def _gdn_kernel(alog_ref, dtb_ref, qkv_ref, z_ref, ab_ref, cbuf_ref, s0_ref, cw_ref, gn_ref,
                o_ref, sfin_ref, xpad, q_scr, k_scr, v_scr, gc_scr, df_scr, be_scr, s_scr, *,
                bb, tb, chunk, heads, dk, dv, cw):
    t = pl.program_id(1)
    tail = cw - 1
    rows_all = bb * tb
    nb = bb * heads

    @pl.when(t == 0)
    def _():
        s_scr[...] = s0_ref[...]
        xpad[:, CONV_PAD - tail:CONV_PAD, :] = cbuf_ref[...]

    xpad[:, CONV_PAD:CONV_PAD + tb, :] = qkv_ref[...]

    def conv_silu(c0, width):
        acc = None
        for i in range(cw):
            r0 = CONV_PAD - tail + i
            term = xpad[:, r0:r0 + tb, c0:c0 + width] * cw_ref[i:i + 1, c0:c0 + width]
            acc = term if acc is None else acc + term
        return _silu(acc)

    def l2n(x):
        return x * lax.rsqrt(jnp.sum(x * x, axis=-1, keepdims=True) + EPS)

    for h in range(heads):
        q_scr[:, h] = l2n(conv_silu(h * dk, dk)) * (dk ** -0.5)
        k_scr[:, h] = l2n(conv_silu(heads * dk + h * dk, dk))
        v_scr[:, h] = conv_silu(2 * heads * dk + h * dv, dv)

    xpad[:, CONV_PAD - tail:CONV_PAD, :] = xpad[:, CONV_PAD + tb - tail:CONV_PAD + tb, :]

    ab = ab_ref[...].reshape(rows_all, LANES)
    g = -jnp.exp(alog_ref[...]) * _softplus(ab + dtb_ref[...])
    beta = jax.nn.sigmoid(ab)
    rr = _iota2((rows_all, rows_all), 0)
    cc = _iota2((rows_all, rows_all), 1)
    l_blk = jnp.where((rr // chunk == cc // chunk) & (rr >= cc), 1.0, 0.0).astype(BF16)
    after = (_iota2((rows_all, LANES), 0) % chunk) > _iota2((rows_all, LANES), 1)
    for h in range(heads):
        g_rep = jnp.broadcast_to(g[:, h:h + 1], (rows_all, LANES))
        rhs = jnp.concatenate([g_rep, jnp.where(after, g_rep, 0.0)], axis=1)
        hi = rhs.astype(BF16)
        lo = (rhs - hi.astype(F32)).astype(BF16)
        res = _dot(l_blk, hi) + _dot(l_blk, lo)
        gc_scr[:, h] = res[:, 0:LANES].reshape(bb, tb, LANES)
        df_scr[:, h] = res[:, LANES:2 * LANES].reshape(bb, tb, LANES)
        be_scr[:, h] = jnp.broadcast_to(beta[:, heads + h:heads + h + 1], (rows_all, LANES)).reshape(
            bb, tb, LANES)

    row = _iota2((chunk, chunk), 0)
    col = _iota2((chunk, chunk), 1)
    lower = row >= col
    strict = row > col
    eye = jnp.where(row == col, 1.0, 0.0)

    def do_chunk(c, _):
        r0 = c * chunk
        rows = pl.ds(r0, chunk)

        def ld(ref):
            return ref[:, :, rows, :].reshape(nb, chunk, LANES)

        q, k, v, gc, df, be = ld(q_scr), ld(k_scr), ld(v_scr), ld(gc_scr), ld(df_scr), ld(be_scr)
        kb = k * be
        decay = jnp.where(lower, jnp.exp(jnp.where(lower, df[:, :, 0:chunk], 0.0)), 0.0)
        kq = _bdot_nt(jnp.concatenate([kb, q], axis=1).astype(BF16), k.astype(BF16))
        m = jnp.where(strict, kq[:, 0:chunk] * decay, 0.0)
        attn = kq[:, chunk:2 * chunk] * decay
        t_inv = _unit_lower_inverse(m, eye)
        eg = jnp.exp(gc)
        sol = _bdot(t_inv.astype(BF16), jnp.concatenate([v * be, kb * eg], axis=2).astype(BF16))
        u_c = sol[:, :, 0:dv]
        w_c = sol[:, :, dv:dv + dk]
        gc_last = gc[:, chunk - 1:chunk, :]
        k_tail = k * jnp.exp(gc_last - gc)
        s = s_scr[...].reshape(nb, dk, dv)
        ws = _bdot(jnp.concatenate([w_c, q * eg], axis=1).astype(BF16), s.astype(BF16))
        v_new = u_c - ws[:, 0:chunk]
        v_new_b = v_new.astype(BF16)
        o = ws[:, chunk:2 * chunk] + _bdot(attn.astype(BF16), v_new_b)
        s_new = s * jnp.exp(gc_last) + _bdot_tn(k_tail.astype(BF16), v_new_b)
        s_scr[...] = s_new.reshape(bb, heads, dk, dv)
        zg = jnp.stack([z_ref[:, rows, h * dv:(h + 1) * dv] for h in range(heads)], axis=1)
        og = (_rms(o, gn_ref[...]) * _silu(zg.reshape(nb, chunk, dv))).reshape(bb, heads, chunk, dv)
        for h in range(heads):
            o_ref[:, rows, h * dv:(h + 1) * dv] = og[:, h]
        return 0

    for c in range(tb // chunk):
        do_chunk(c, 0)

    @pl.when(t == pl.num_programs(1) - 1)
    def _():
        sfin_ref[...] = s_scr[...]


def _gdn(qkv, z, ab, conv_buf, s0, conv_w, a_log_row, dt_bias_row, gn_w, layer, batch, seq, chunk_pref,
         bb):
    n, dconv = qkv.shape
    _, heads, dk, dv = s0.shape
    assert dk == LANES and dv == LANES
    cw = conv_w.shape[1]
    tb = _tile(seq, 256)
    chunk = _tile(tb, chunk_pref)
    nt = seq // tb
    dz = heads * dv
    assert batch % bb == 0
    row_map = lambda b, t: (b, t, 0)
    state_map = lambda b, t: (b, 0, 0, 0)
    og, s_new = pl.pallas_call(
        functools.partial(_gdn_kernel, bb=bb, tb=tb, chunk=chunk, heads=heads, dk=dk, dv=dv, cw=cw),
        grid=(batch // bb, nt),
        in_specs=[
            pl.BlockSpec((None, 1, LANES), lambda b, t: (layer, 0, 0)),
            pl.BlockSpec((None, 1, LANES), lambda b, t: (layer, 0, 0)),
            pl.BlockSpec((bb, tb, dconv), row_map),
            pl.BlockSpec((bb, tb, dz), row_map),
            pl.BlockSpec((bb, tb, LANES), row_map),
            pl.BlockSpec((bb, cw - 1, dconv), lambda b, t: (b, 0, 0)),
            pl.BlockSpec((bb, heads, dk, dv), state_map),
            pl.BlockSpec((None, cw, dconv), lambda b, t: (layer, 0, 0)),
            pl.BlockSpec((None, 1, dv), lambda b, t: (layer, 0, 0)),
        ],
        out_specs=[
            pl.BlockSpec((bb, tb, dz), row_map),
            pl.BlockSpec((bb, heads, dk, dv), state_map),
        ],
        out_shape=[
            jax.ShapeDtypeStruct((batch, seq, dz), F32),
            jax.ShapeDtypeStruct((batch, heads, dk, dv), F32),
        ],
        scratch_shapes=[pltpu.VMEM((bb, CONV_PAD + tb, dconv), F32)]
        + [pltpu.VMEM((bb, heads, tb, LANES), F32)] * 6
        + [pltpu.VMEM((bb, heads, dk, dv), F32)],
        compiler_params=_cparams("parallel", "arbitrary"),
        name="gdn",
    )(a_log_row, dt_bias_row, qkv.reshape(batch, seq, dconv), z.reshape(batch, seq, dz),
      ab.reshape(batch, seq, LANES), conv_buf, s0, conv_w, gn_w)
    return og.reshape(n, dz), s_new


def _out_proj_stage(x_ref, y_ref, w_ref):
    return x_ref[...] + _dot(y_ref[...].astype(BF16), w_ref[...])


def _pre_out_proj(x, y, w, layer):
    n, d = x.shape
    tm = _tile(n, FFN_ROWS)
    specs = [
        pl.BlockSpec((tm, d), lambda i: (i, 0)),
        pl.BlockSpec((tm, y.shape[1]), lambda i: (i, 0)),
        _resident((None, y.shape[1], d), lambda i: (layer, 0, 0)),
    ]
    return ([x, y, w], specs, _out_proj_stage, [])


def _trunk(x, batch, seq, pos0, cache, pool_bufs, conv_bufs, rec_states, w):
    depth = w["norm_mix"].shape[0]
    heads_a, dh = w["heads_a"], w["dh"]
    da = heads_a * dh
    new_k, new_v, new_pool, new_conv, new_rec = [], [], [], [], []
    for layer in range(depth):
        i = layer // 2
        n, d = x.shape
        x = _ffn(_pre_plain(x), n, d, w["norm_ffn"], w["w_ffn_in"], w["w_ffn_out"], layer, 0)
        if layer % 2 == 0:
            db_ = w["w_in_even"].shape[2] - 3 * da
            if cache is None:
                q, u, kt, vt, kt_bf, vt_bf = _proj_even_prompt(
                    x, w["norm_mix"], layer, w["w_in_even"], w["w_kv_even_t"], i, batch, seq, da)
                oa = _attn_prompt(q, kt_bf, vt_bf, w["sb_bias"][i][None], batch, seq, dh)
                mixed = _pre_even_prompt(x, oa, u, w["w_pool"], w["pool_scale"], w["w_out_even"], i, seq)
                new_pool.append(u.reshape(batch, seq, db_)[:, seq - (POOL_HALO - 1):])
                new_k.append(jnp.transpose(kt.reshape(batch, heads_a, dh, seq), (0, 3, 1, 2)))
                new_v.append(jnp.transpose(vt.reshape(batch, heads_a, dh, seq), (0, 3, 1, 2)))
            else:
                segs = [(0, da, (0,)), (da, 2 * da, (1,)), (2 * da, 3 * da, (2,)),
                        (3 * da, 3 * da + db_, (3,))]
                q, k, v, u = _proj(x, (w["norm_mix"], layer), (w["w_in_even"], i), segs, [F32] * 4,
                                   [da, da, da, db_], "in_even")
                new_k.append(k.reshape(batch, seq, heads_a, dh))
                new_v.append(v.reshape(batch, seq, heads_a, dh))
                cache_k, cache_v, page_table = cache
                oa = _attn_sample(q, k, v, cache_k, cache_v, i, page_table, w["sb_bias"][i][None], seq)
                u3 = u.reshape(batch, seq, db_)
                buf = pool_bufs[i]
                padded = jnp.concatenate(
                    [jnp.zeros((batch, POOL_HALO - buf.shape[1], db_), F32), buf, u3], axis=1)
                mixed = _pre_even_sample(x, oa, padded, w["w_pool"], w["pool_scale"], w["w_out_even"], i,
                                         seq, pos0)
                new_pool.append(padded[:, padded.shape[1] - buf.shape[1]:])
        else:
            heads_c, dk, dv = rec_states.shape[2:]
            dconv = heads_c * (2 * dk + dv)
            dz = heads_c * dv
            segs = [(0, dconv, (0,)), (dconv, dconv + dz, (1,)), (dconv + dz, dconv + dz + LANES, (2,))]
            qkv, z, ab = _proj(x, (w["norm_mix"], layer), (w["w_in_odd"], i), segs, [F32] * 3,
                               [dconv, dz, LANES], "in_odd")
            og, s_new = _gdn(qkv, z, ab, conv_bufs[i], rec_states[i], w["conv_w"], w["a_log"],
                             w["dt_bias"], w["gn_w"], i, batch, seq, 64, 1 if cache is None else 4)
            mixed = _pre_out_proj(x, og, w["w_out_odd"], i)
            tail = conv_bufs.shape[2]
            xp = jnp.concatenate([conv_bufs[i], qkv.reshape(batch, seq, dconv)], axis=1)
            new_conv.append(xp[:, xp.shape[1] - tail:])
            new_rec.append(s_new)
        final_g = w["norm_final"] if layer == depth - 1 else None
        x = _ffn(mixed, n, d, w["norm_ffn"], w["w_ffn_in"], w["w_ffn_out"], layer, 1, final_g)
    return x, jnp.stack(new_k), jnp.stack(new_v), jnp.stack(new_pool), jnp.stack(new_conv), jnp.stack(new_rec)


def kernel(x_prompt, x_sample, cache_k, cache_v, page_table, state_pool, state_conv, state_rec, norm_ffn,
           w_ffn_in, w_ffn_out, norm_mix, w_in_even, sb_bias, w_pool, pool_scale, w_out_even, w_in_odd,
           conv_w, a_log, dt_bias, gn_w, w_out_odd, norm_final):
    bp, seq, d = x_prompt.shape
    db, dseq, _ = x_sample.shape
    n_even, n_pool, page, heads_a, dh = cache_k.shape
    n_odd, _, heads_c, dk, dv = state_rec.shape
    da = heads_a * dh
    dconv = heads_c * (2 * dk + dv)
    dz = heads_c * dv
    depth = norm_mix.shape[0]

    def lane_row(p):
        return jnp.pad(p.astype(F32), ((0, 0), (0, LANES - p.shape[1])))[:, None, :]

    w_odd = jnp.concatenate(
        [w_in_odd, jnp.zeros((n_odd, d, LANES - 2 * heads_c), w_in_odd.dtype)], axis=2).astype(BF16)
    w = dict(
        heads_a=heads_a, dh=dh,
        norm_ffn=norm_ffn.reshape(depth, 2, 1, d),
        w_ffn_in=w_ffn_in.astype(BF16),
        w_ffn_out=w_ffn_out.astype(BF16),
        norm_mix=norm_mix.reshape(depth, 1, d),
        w_in_even=w_in_even.astype(BF16),
        w_kv_even_t=jnp.transpose(w_in_even[:, :, da:3 * da], (0, 2, 1)).astype(BF16),
        sb_bias=sb_bias.astype(F32),
        w_pool=w_pool.astype(BF16),
        pool_scale=pool_scale.reshape(n_even, 1, -1),
        w_out_even=w_out_even.astype(BF16),
        w_in_odd=w_odd,
        conv_w=conv_w,
        a_log=lane_row(a_log),
        dt_bias=lane_row(dt_bias),
        gn_w=gn_w.reshape(n_odd, 1, dv),
        w_out_odd=w_out_odd.astype(BF16),
        norm_final=norm_final.reshape(1, d),
    )
    past_len = page_table.shape[1] * page

    yp, kp, vp, poolp, convp, recp = _trunk(
        x_prompt.reshape(bp * seq, d), bp, seq, 0, None,
        None,
        jnp.zeros((n_odd, bp, conv_w.shape[1] - 1, dconv), state_conv.dtype),
        jnp.zeros((n_odd, bp, heads_c, dk, dv), state_rec.dtype), w)
    ys, ks, vs, pools, convs, recs = _trunk(
        x_sample.reshape(db * dseq, d), db, dseq, past_len, (cache_k, cache_v, page_table),
        state_pool, state_conv, state_rec, w)
    return (yp.reshape(bp, seq, d), ys.reshape(db, dseq, d), kp, vp, ks, vs, poolp, pools, convp, convs,
            recp, recs)
```

```python
import functools

import jax
import jax.numpy as jnp
from jax import lax
from jax.experimental import pallas as pl
from jax.experimental.pallas import tpu as pltpu

F32 = jnp.float32
BF16 = jnp.bfloat16
EPS = 1e-6
FFN_HALF = 0.5
POOL_WINDOWS = (2, 4, 8, 16)
POOL_HALO = 16
LANES = 128
SUBLANES = 8
VMEM_LIMIT = 56 * 1024 * 1024
NEG_LOG2E = -1.4426950408889634


def _cparams(*sem):
    return pltpu.CompilerParams(dimension_semantics=sem, vmem_limit_bytes=VMEM_LIMIT)


def _tile(n, pref):
    t = pref
    while t > SUBLANES and n % t:
        t //= 2
    assert n % t == 0, (n, pref)
    return t


def _dot(a, b):
    return jnp.dot(a, b, preferred_element_type=F32)


def _dot_nt(a, b):
    return lax.dot_general(a, b, (((1,), (1,)), ((), ())), preferred_element_type=F32)


def _rms(x, g):
    return x * lax.rsqrt(jnp.mean(x * x, axis=-1, keepdims=True) + EPS) * g


def _softplus(x):
    return jnp.maximum(x, 0.0) + jnp.log(1.0 + jnp.exp2(jnp.abs(x) * NEG_LOG2E))


def _silu(x):
    return x * jax.nn.sigmoid(x)


def _iota2(shape, dim):
    return lax.broadcasted_iota(jnp.int32, shape, dim)


def _resident(shape, index_map):
    return pl.BlockSpec(shape, index_map, pipeline_mode=pl.Buffered(1))


def _ffn_kernel(*refs, n_pre, n_scratch, pre_fn, d_ff, ck, final):
    pre_refs = refs[:n_pre]
    scratch = refs[len(refs) - n_scratch:]
    g_ref, wi_ref, wo_ref, *rest = refs[n_pre:len(refs) - n_scratch]
    o_ref = rest[-1]
    x = pre_fn(*pre_refs, *scratch)
    h = _rms(x, g_ref[...]).astype(BF16)
    acc = None
    for c in range(d_ff // ck):
        gate = _dot(h, wi_ref[:, c * ck:(c + 1) * ck])
        up = _dot(h, wi_ref[:, d_ff + c * ck:d_ff + (c + 1) * ck])
        act = (_silu(gate) * up).astype(BF16)
        part = _dot(act, wo_ref[c * ck:(c + 1) * ck, :])
        acc = part if acc is None else acc + part
    y = x + FFN_HALF * acc
    if final:
        y = _rms(y, rest[0][...])
    o_ref[...] = y


FFN_ROWS = 512


def _ffn(pre, n, d, norm_ffn, w_in, w_out, layer, j, final_g=None):
    pre_args, pre_specs, pre_fn, scratch_shapes = pre
    d_ff = w_out.shape[2]
    tm = _tile(n, FFN_ROWS)
    ck = 256 if d_ff % 256 == 0 else d_ff
    in_specs = list(pre_specs) + [
        pl.BlockSpec((None, None, 1, d), lambda i: (layer, j, 0, 0)),
        _resident((None, None, d, 2 * d_ff), lambda i: (layer, j, 0, 0)),
        _resident((None, None, d_ff, d), lambda i: (layer, j, 0, 0)),
    ]
    args = list(pre_args) + [norm_ffn, w_in, w_out]
    if final_g is not None:
        in_specs.append(pl.BlockSpec((1, d), lambda i: (0, 0)))
        args.append(final_g)
    return pl.pallas_call(
        functools.partial(_ffn_kernel, n_pre=len(pre_args), n_scratch=len(scratch_shapes), pre_fn=pre_fn,
                          d_ff=d_ff, ck=ck, final=final_g is not None),
        grid=(n // tm,),
        in_specs=in_specs,
        out_specs=pl.BlockSpec((tm, d), lambda i: (i, 0)),
        out_shape=jax.ShapeDtypeStruct((n, d), F32),
        scratch_shapes=list(scratch_shapes),
        compiler_params=_cparams("parallel"),
        name="ffn",
    )(*args)


def _pre_plain(x):
    n, d = x.shape
    tm = _tile(n, FFN_ROWS)
    return ([x], [pl.BlockSpec((tm, d), lambda i: (i, 0))], lambda x_ref: x_ref[...], [])


def _proj_kernel(x_ref, g_ref, w_ref, *o_refs, segs):
    h = _rms(x_ref[...], g_ref[...]).astype(BF16)
    for (a, b, outs) in segs:
        y = _dot(h, w_ref[:, a:b])
        for oi in outs:
            o_refs[oi][...] = y.astype(o_refs[oi].dtype)


def _proj(x, g, w, segs, out_dtypes, out_widths, name):
    n, d = x.shape
    tm = _tile(n, 512)
    g_arr, g_idx = g
    w_arr, w_idx = w
    in_specs = [
        pl.BlockSpec((tm, d), lambda i: (i, 0)),
        pl.BlockSpec((None, 1, d), lambda i: (g_idx, 0, 0)),
        _resident((None, d, w_arr.shape[2]), lambda i: (w_idx, 0, 0)),
    ]
    return pl.pallas_call(
        functools.partial(_proj_kernel, segs=segs),
        grid=(n // tm,),
        in_specs=in_specs,
        out_specs=[pl.BlockSpec((tm, wd), lambda i: (i, 0)) for wd in out_widths],
        out_shape=[jax.ShapeDtypeStruct((n, wd), dt) for wd, dt in zip(out_widths, out_dtypes)],
        compiler_params=_cparams("parallel"),
        name=name,
    )(x, g_arr, w_arr)


def _proj_even_prompt_kernel(x_ref, g_ref, w_ref, wkv_t_ref, q_ref, u_ref, kt_ref, vt_ref, ktb_ref, vtb_ref,
                             *, da):
    h = _rms(x_ref[...], g_ref[...]).astype(BF16)
    q_ref[...] = _dot(h, w_ref[:, 0:da])
    u_ref[...] = _dot(h, w_ref[:, 3 * da:])
    kv_t = _dot_nt(wkv_t_ref[...], h)
    kt_ref[...] = kv_t[0:da]
    vt_ref[...] = kv_t[da:2 * da]
    ktb_ref[...] = kv_t[0:da].astype(BF16)
    vtb_ref[...] = kv_t[da:2 * da].astype(BF16)


def _proj_even_prompt(x, norm_mix, layer, w, wkv_t, i, batch, seq, da):
    n, d = x.shape
    tm = _tile(seq, 512)
    per_seq = seq // tm
    db_ = w.shape[2] - 3 * da
    row_spec = lambda wd: pl.BlockSpec((tm, wd), lambda t: (t, 0))
    lane_spec = pl.BlockSpec((None, da, tm), lambda t: (t // per_seq, 0, t % per_seq))
    return pl.pallas_call(
        functools.partial(_proj_even_prompt_kernel, da=da),
        grid=(n // tm,),
        in_specs=[
            row_spec(d),
            pl.BlockSpec((None, 1, d), lambda t: (layer, 0, 0)),
            _resident((None, d, w.shape[2]), lambda t: (i, 0, 0)),
            _resident((None, 2 * da, d), lambda t: (i, 0, 0)),
        ],
        out_specs=[row_spec(da), row_spec(db_), lane_spec, lane_spec, lane_spec, lane_spec],
        out_shape=[
            jax.ShapeDtypeStruct((n, da), F32),
            jax.ShapeDtypeStruct((n, db_), F32),
            jax.ShapeDtypeStruct((batch, da, seq), F32),
            jax.ShapeDtypeStruct((batch, da, seq), F32),
            jax.ShapeDtypeStruct((batch, da, seq), BF16),
            jax.ShapeDtypeStruct((batch, da, seq), BF16),
        ],
        compiler_params=_cparams("parallel"),
        name="in_even_prompt",
    )(x, norm_mix, w, wkv_t)


def _suffix_matrix(n):
    row = _iota2((2 * n, n), 0) % n
    col = _iota2((2 * n, n), 1)
    return jnp.where(row > col, 1.0, 0.0).astype(BF16)


ATTN_LANE_BLOCKS = 2


def _attn_prompt_kernel(bias_ref, q_ref, k_ref, v_ref, o_ref, r_scr, acc_scr, *, tq, dh, scale, groups):
    hg = pl.program_id(1)
    i = pl.program_id(2)
    nh = LANES // dh
    lane = _iota2((tq, LANES), 1)
    heads = [(lane >= e * dh) & (lane < (e + 1) * dh) for e in range(nh)]
    u_tri = _suffix_matrix(tq)
    causal = _iota2((nh * tq, tq), 1) < (_iota2((nh * tq, tq), 0) % tq)
    qs, biases = [], []
    for g in range(groups):
        q = q_ref[:, g * LANES:(g + 1) * LANES] * scale
        qs.append(jnp.concatenate([jnp.where(hm, q, 0.0) for hm in heads], axis=0).astype(BF16))
        biases.append([bias_ref[0, (hg * groups + g) * nh + e] for e in range(nh)])

    def tile(j, valid):
        r0 = pl.multiple_of(j * tq, tq)
        gs = range(groups)
        zs = [_dot(qs[g], k_ref[g * LANES:(g + 1) * LANES, pl.ds(r0, tq)]) for g in gs]
        zs = [jnp.concatenate([zs[g][e * tq:(e + 1) * tq] + biases[g][e] for e in range(nh)], axis=0)
              for g in gs]
        sps = [_softplus(z) for z in zs]
        logsig = [z - sp for z, sp in zip(zs, sps)]
        drops = sps if valid is None else [jnp.where(valid, sp, 0.0) for sp in sps]
        his = [d.astype(BF16) for d in drops]
        los = [(d - hi.astype(F32)).astype(BF16) for d, hi in zip(drops, his)]
        laters = [_dot(jnp.concatenate([hi, lo], axis=1), u_tri) for hi, lo in zip(his, los)]
        for g in gs:
            a = jnp.exp(logsig[g] - laters[g] - r_scr[g])
            if valid is not None:
                a = jnp.where(valid, a, 0.0)
            acc_scr[g] += _dot_nt(a.astype(BF16), v_ref[g * LANES:(g + 1) * LANES, pl.ds(r0, tq)])
            r_scr[g] += jnp.sum(drops[g], axis=1, keepdims=True)

    r_scr[...] = jnp.zeros_like(r_scr)
    acc_scr[...] = jnp.zeros_like(acc_scr)
    tile(i, causal)

    def body(s, c):
        tile(i - 1 - s, None)
        return c

    lax.fori_loop(0, i, body, 0)
    for g in range(groups):
        out = jnp.where(heads[0], acc_scr[g, 0:tq], 0.0)
        for e in range(1, nh):
            out = out + jnp.where(heads[e], acc_scr[g, e * tq:(e + 1) * tq], 0.0)
        o_ref[:, g * LANES:(g + 1) * LANES] = out


def _attn_prompt(q, k_bf, v_bf, sb_bias, batch, seq, dh):
    n, da = q.shape
    tq = _tile(seq, 256)
    nq = seq // tq
    groups = ATTN_LANE_BLOCKS if da % (ATTN_LANE_BLOCKS * LANES) == 0 else 1
    wl = groups * LANES
    rows = LANES // dh * tq
    return pl.pallas_call(
        functools.partial(_attn_prompt_kernel, tq=tq, dh=dh, scale=dh ** -0.5, groups=groups),
        grid=(batch, da // wl, nq),
        in_specs=[
            pl.BlockSpec(memory_space=pltpu.SMEM),
            pl.BlockSpec((tq, wl), lambda b, hg, i: (b * nq + i, hg)),
            pl.BlockSpec((None, wl, seq), lambda b, hg, i: (b, hg, 0)),
            pl.BlockSpec((None, wl, seq), lambda b, hg, i: (b, hg, 0)),
        ],
        out_specs=pl.BlockSpec((tq, wl), lambda b, hg, i: (b * nq + i, hg)),
        out_shape=jax.ShapeDtypeStruct((n, da), F32),
        scratch_shapes=[pltpu.VMEM((groups, rows, 1), F32), pltpu.VMEM((groups, rows, LANES), F32)],
        compiler_params=_cparams("parallel", "parallel", "arbitrary"),
        name="attn_prompt",
    )(sb_bias, q, k_bf, v_bf)


def _attn_sample_kernel(pt_ref, bias_ref, q_ref, kn_ref, vn_ref, *rest, nseq, n_pages, page, t_new, heads,
                        dh, scale):
    del pt_ref
    k_pages = [rest[s * n_pages:(s + 1) * n_pages] for s in range(nseq)]
    v_pages = [rest[(nseq + s) * n_pages:(nseq + s + 1) * n_pages] for s in range(nseq)]
    o_ref, r_scr, acc_scr = rest[2 * nseq * n_pages:]
    j = pl.program_id(1)
    ht = heads * t_new
    seqs = range(nseq)
    hs = range(heads)
    q = q_ref[...] * scale
    qh = [[q[s * t_new:(s + 1) * t_new, h * dh:(h + 1) * dh].astype(BF16) for h in hs] for s in seqs]
    u_tri = _suffix_matrix(page)

    def tile(logits, weighted, valid, slots):
        zs = [jnp.concatenate([logits(s, h) + bias_ref[0, h] for h in hs], axis=0) for s in seqs]
        sps = [_softplus(z) for z in zs]
        logsig = [z - sp for z, sp in zip(zs, sps)]
        drops = sps if valid is None else [jnp.where(valid, sp, 0.0) for sp in sps]
        his = [d.astype(BF16) for d in drops]
        los = [(d - hi.astype(F32)).astype(BF16) for d, hi in zip(drops, his)]
        cols = [slice(p * page, (p + 1) * page) for p in range(slots)]
        later = _dot(jnp.concatenate(
            [jnp.concatenate([his[s][:, c], los[s][:, c]], axis=1) for s in seqs for c in cols], axis=0),
            u_tri)
        for s in seqs:
            r_run = r_scr[s]
            shift = []
            for p, c in enumerate(cols):
                r0 = (s * slots + p) * ht
                shift.append(later[r0:r0 + ht] + r_run)
                r_run = r_run + jnp.sum(drops[s][:, c], axis=1, keepdims=True)
            r_scr[s] = r_run
            a = jnp.exp(logsig[s] - jnp.concatenate(shift, axis=1))
            if valid is not None:
                a = jnp.where(valid, a, 0.0)
            acc_scr[s] += jnp.concatenate(
                [weighted(s, h, a[h * t_new:(h + 1) * t_new].astype(BF16)) for h in hs], axis=0)

    @pl.when(j == 0)
    def _():
        r_scr[...] = jnp.zeros_like(r_scr)
        acc_scr[...] = jnp.zeros_like(acc_scr)
        pad = jnp.zeros((page - t_new, dh), F32)
        kn = kn_ref[...]
        vn = vn_ref[...]
        valid_new = _iota2((ht, page), 1) < (_iota2((ht, page), 0) % t_new)

        def padded(x, s, h):
            return jnp.concatenate([x[s * t_new:(s + 1) * t_new, h * dh:(h + 1) * dh], pad],
                                   axis=0).astype(BF16)

        tile(lambda s, h: _dot_nt(qh[s][h], padded(kn, s, h)), lambda s, h, a: _dot(a, padded(vn, s, h)),
             valid_new, 1)

    def lanes_of(refs, h):
        return jnp.concatenate([r[h] for r in refs], axis=1).astype(BF16)

    tile(lambda s, h: _dot(qh[s][h], lanes_of(k_pages[s], h)),
         lambda s, h, a: _dot_nt(a, lanes_of(v_pages[s], h)), None, n_pages)

    @pl.when(j == pl.num_programs(1) - 1)
    def _():
        for s in seqs:
            o_ref[s * t_new:(s + 1) * t_new, :] = jnp.concatenate(
                [acc_scr[s, h * t_new:(h + 1) * t_new, :] for h in hs], axis=1)


SAMPLE_PAGES_PER_STEP = 16
SAMPLE_SEQS_PER_STEP = 2


def _attn_sample(q, k_new, v_new, cache_k, cache_v, layer, page_table, sb_bias, t_new):
    n, da = q.shape
    db, n_pages = page_table.shape
    _, _, page, heads, dh = cache_k.shape
    cache_k = jnp.transpose(cache_k, (0, 1, 3, 4, 2))
    cache_v = jnp.transpose(cache_v, (0, 1, 3, 4, 2))
    pps = SAMPLE_PAGES_PER_STEP if n_pages % SAMPLE_PAGES_PER_STEP == 0 else n_pages
    steps = n_pages // pps
    nseq = SAMPLE_SEQS_PER_STEP if db % SAMPLE_SEQS_PER_STEP == 0 else 1
    page_specs = [
        pl.BlockSpec((None, None, heads, dh, page),
                     lambda b, j, pt, s=s, p=p: (layer, pt[b * nseq + s, n_pages - 1 - (j * pps + p)],
                                                 0, 0, 0))
        for s in range(nseq) for p in range(pps)
    ]
    row_spec = pl.BlockSpec((nseq * t_new, da), lambda b, j, pt: (b, 0))
    grid_spec = pltpu.PrefetchScalarGridSpec(
        num_scalar_prefetch=1,
        grid=(db // nseq, steps),
        in_specs=[pl.BlockSpec(memory_space=pltpu.SMEM), row_spec, row_spec, row_spec]
        + page_specs + page_specs,
        out_specs=row_spec,
        scratch_shapes=[pltpu.VMEM((nseq, heads * t_new, 1), F32),
                        pltpu.VMEM((nseq, heads * t_new, dh), F32)],
    )
    n_blocks = nseq * pps
    return pl.pallas_call(
        functools.partial(_attn_sample_kernel, nseq=nseq, n_pages=pps, page=page, t_new=t_new, heads=heads,
                          dh=dh, scale=dh ** -0.5),
        grid_spec=grid_spec,
        out_shape=jax.ShapeDtypeStruct((n, da), F32),
        compiler_params=_cparams("parallel", "arbitrary"),
        name="attn_sample",
    )(page_table, sb_bias, q, k_new, v_new, *([cache_k] * n_blocks), *([cache_v] * n_blocks))


def _even_out_tail(x, oa, diffs, wp_ref, ps_ref, wo_ref):
    mixed = [oa.astype(BF16)]
    for g, diff in enumerate(diffs):
        cb = diff.shape[1]
        ob = _dot(diff.astype(BF16), wp_ref[g]) * ps_ref[:, g * cb:(g + 1) * cb]
        mixed.append(ob.astype(BF16))
    return x + _dot(jnp.concatenate(mixed, axis=1), wo_ref[...])


def _even_out_prompt_stage(x_ref, oa_ref, u_ref, halo_ref, wp_ref, ps_ref, wo_ref, pad_scr, *,
                           tp, tiles_per_seq, cb):
    it = pl.program_id(0) % tiles_per_seq
    pad_scr[0:POOL_HALO, :] = jnp.where(it == 0, 0.0, halo_ref[...])
    pad_scr[POOL_HALO:POOL_HALO + tp, :] = u_ref[...]
    pos = it * tp + _iota2((tp, 1), 0)
    diffs = []
    for g, w in enumerate(POOL_WINDOWS):
        cols = slice(g * cb, (g + 1) * cb)
        s = pad_scr[POOL_HALO:POOL_HALO + tp, cols]
        for j in range(1, w):
            s = s + pad_scr[POOL_HALO - j:POOL_HALO - j + tp, cols]
        cnt = jnp.minimum(w, pos + 1).astype(F32)
        diffs.append(s / cnt - u_ref[:, cols])
    return _even_out_tail(x_ref[...], oa_ref[...], diffs, wp_ref, ps_ref, wo_ref)


def _pre_even_prompt(x, oa, u, w_pool, pool_scale, w_out, layer, seq):
    n, d = x.shape
    db_ = u.shape[1]
    tp = _tile(n, FFN_ROWS)
    assert seq % tp == 0 and tp % POOL_HALO == 0
    tiles_per_seq = seq // tp
    hb = tp // POOL_HALO
    g_b = w_pool.shape[1]
    cb = db_ // g_b
    specs = [
        pl.BlockSpec((tp, d), lambda i: (i, 0)),
        pl.BlockSpec((tp, oa.shape[1]), lambda i: (i, 0)),
        pl.BlockSpec((tp, db_), lambda i: (i, 0)),
        pl.BlockSpec((POOL_HALO, db_), lambda i: (jnp.maximum(i * hb - 1, 0), 0)),
        _resident((None, g_b, cb, cb), lambda i: (layer, 0, 0, 0)),
        pl.BlockSpec((None, 1, db_), lambda i: (layer, 0, 0)),
        _resident((None, d, d), lambda i: (layer, 0, 0)),
    ]
    fn = functools.partial(_even_out_prompt_stage, tp=tp, tiles_per_seq=tiles_per_seq, cb=cb)
    return ([x, oa, u, u, w_pool, pool_scale, w_out], specs, fn, [pltpu.VMEM((POOL_HALO + tp, db_), F32)])


def _even_out_sample_stage(x_ref, oa_ref, p_ref, wp_ref, ps_ref, wo_ref, *, bb, t_new, pos0, cb):
    pos = pos0 + _iota2((1, t_new, 1), 1)
    diffs = []
    for g, w in enumerate(POOL_WINDOWS):
        cols = slice(g * cb, (g + 1) * cb)
        u3 = p_ref[:, POOL_HALO:POOL_HALO + t_new, cols]
        s = u3
        for j in range(1, w):
            s = s + p_ref[:, POOL_HALO - j:POOL_HALO - j + t_new, cols]
        cnt = jnp.minimum(w, pos + 1).astype(F32)
        diffs.append((s / cnt - u3).reshape(bb * t_new, cb))
    return _even_out_tail(x_ref[...], oa_ref[...], diffs, wp_ref, ps_ref, wo_ref)


def _pre_even_sample(x, oa, padded, w_pool, pool_scale, w_out, layer, t_new, pos0):
    n, d = x.shape
    nb, rows, db_ = padded.shape
    tm = _tile(n, FFN_ROWS)
    assert tm % t_new == 0
    bb = tm // t_new
    g_b = w_pool.shape[1]
    cb = db_ // g_b
    specs = [
        pl.BlockSpec((tm, d), lambda i: (i, 0)),
        pl.BlockSpec((tm, oa.shape[1]), lambda i: (i, 0)),
        pl.BlockSpec((bb, rows, db_), lambda i: (i, 0, 0)),
        _resident((None, g_b, cb, cb), lambda i: (layer, 0, 0, 0)),
        pl.BlockSpec((None, 1, db_), lambda i: (layer, 0, 0)),
        _resident((None, d, d), lambda i: (layer, 0, 0)),
    ]
    fn = functools.partial(_even_out_sample_stage, bb=bb, t_new=t_new, pos0=pos0, cb=cb)
    return ([x, oa, padded, w_pool, pool_scale, w_out], specs, fn, [])


CONV_PAD = SUBLANES


def _bdot(a, b):
    return lax.dot_general(a, b, (((2,), (1,)), ((0,), (0,))), preferred_element_type=F32)


def _bdot_nt(a, b):
    return lax.dot_general(a, b, (((2,), (2,)), ((0,), (0,))), preferred_element_type=F32)


def _bdot_tn(a, b):
    return lax.dot_general(a, b, (((1,), (1,)), ((0,), (0,))), preferred_element_type=F32)


def _unit_lower_inverse(m, eye):
    n = m.shape[-1]
    p = eye - m
    x = m
    k = 2
    while k < n:
        xb = x.astype(BF16)
        x = _bdot(xb, xb)
        p = p + _bdot(p.astype(BF16), x.astype(BF16))
        k *= 2
    return p


def _gdn_kernel(alog_ref, dtb_ref, qkv_ref, z_ref, ab_ref, cbuf_ref, s0_ref, cw_ref, gn_ref,
                o_ref, sfin_ref, xpad, q_scr, k_scr, v_scr, gc_scr, df_scr, be_scr, s_scr, *,
                bb, tb, chunk, heads, dk, dv, cw):
    t = pl.program_id(1)
    tail = cw - 1
    rows_all = bb * tb
    nb = bb * heads

    @pl.when(t == 0)
    def _():
        s_scr[...] = s0_ref[...]
        xpad[:, CONV_PAD - tail:CONV_PAD, :] = cbuf_ref[...]

    xpad[:, CONV_PAD:CONV_PAD + tb, :] = qkv_ref[...]

    def conv_silu(c0, width):
        acc = None
        for i in range(cw):
            r0 = CONV_PAD - tail + i
            term = xpad[:, r0:r0 + tb, c0:c0 + width] * cw_ref[i:i + 1, c0:c0 + width]
            acc = term if acc is None else acc + term
        return _silu(acc)

    def l2n(x):
        return x * lax.rsqrt(jnp.sum(x * x, axis=-1, keepdims=True) + EPS)

    for h in range(heads):
        q_scr[:, h] = l2n(conv_silu(h * dk, dk)) * (dk ** -0.5)
        k_scr[:, h] = l2n(conv_silu(heads * dk + h * dk, dk))
        v_scr[:, h] = conv_silu(2 * heads * dk + h * dv, dv)

    xpad[:, CONV_PAD - tail:CONV_PAD, :] = xpad[:, CONV_PAD + tb - tail:CONV_PAD + tb, :]

    ab = ab_ref[...].reshape(rows_all, LANES)
    g = -jnp.exp(alog_ref[...]) * _softplus(ab + dtb_ref[...])
    beta = jax.nn.sigmoid(ab)
    rr = _iota2((rows_all, rows_all), 0)
    cc = _iota2((rows_all, rows_all), 1)
    l_blk = jnp.where((rr // chunk == cc // chunk) & (rr >= cc), 1.0, 0.0).astype(BF16)
    after = (_iota2((rows_all, LANES), 0) % chunk) > _iota2((rows_all, LANES), 1)
    for h in range(heads):
        g_rep = jnp.broadcast_to(g[:, h:h + 1], (rows_all, LANES))
        rhs = jnp.concatenate([g_rep, jnp.where(after, g_rep, 0.0)], axis=1)
        hi = rhs.astype(BF16)
        lo = (rhs - hi.astype(F32)).astype(BF16)
        res = _dot(l_blk, hi) + _dot(l_blk, lo)
        gc_scr[:, h] = res[:, 0:LANES].reshape(bb, tb, LANES)
        df_scr[:, h] = res[:, LANES:2 * LANES].reshape(bb, tb, LANES)
        be_scr[:, h] = jnp.broadcast_to(beta[:, heads + h:heads + h + 1], (rows_all, LANES)).reshape(
            bb, tb, LANES)

    row = _iota2((chunk, chunk), 0)
    col = _iota2((chunk, chunk), 1)
    lower = row >= col
    strict = row > col
    eye = jnp.where(row == col, 1.0, 0.0)

    def do_chunk(c, _):
        r0 = c * chunk
        rows = pl.ds(r0, chunk)

        def ld(ref):
            return ref[:, :, rows, :].reshape(nb, chunk, LANES)

        q, k, v, gc, df, be = ld(q_scr), ld(k_scr), ld(v_scr), ld(gc_scr), ld(df_scr), ld(be_scr)
        kb = k * be
        decay = jnp.where(lower, jnp.exp(jnp.where(lower, df[:, :, 0:chunk], 0.0)), 0.0)
        kq = _bdot_nt(jnp.concatenate([kb, q], axis=1).astype(BF16), k.astype(BF16))
        m = jnp.where(strict, kq[:, 0:chunk] * decay, 0.0)
        attn = kq[:, chunk:2 * chunk] * decay
        t_inv = _unit_lower_inverse(m, eye)
        eg = jnp.exp(gc)
        sol = _bdot(t_inv.astype(BF16), jnp.concatenate([v * be, kb * eg], axis=2).astype(BF16))
        u_c = sol[:, :, 0:dv]
        w_c = sol[:, :, dv:dv + dk]
        gc_last = gc[:, chunk - 1:chunk, :]
        k_tail = k * jnp.exp(gc_last - gc)
        s = s_scr[...].reshape(nb, dk, dv)
        ws = _bdot(jnp.concatenate([w_c, q * eg], axis=1).astype(BF16), s.astype(BF16))
        v_new = u_c - ws[:, 0:chunk]
        v_new_b = v_new.astype(BF16)
        o = ws[:, chunk:2 * chunk] + _bdot(attn.astype(BF16), v_new_b)
        s_new = s * jnp.exp(gc_last) + _bdot_tn(k_tail.astype(BF16), v_new_b)
        s_scr[...] = s_new.reshape(bb, heads, dk, dv)
        zg = jnp.stack([z_ref[:, rows, h * dv:(h + 1) * dv] for h in range(heads)], axis=1)
        og = (_rms(o, gn_ref[...]) * _silu(zg.reshape(nb, chunk, dv))).reshape(bb, heads, chunk, dv)
        for h in range(heads):
            o_ref[:, rows, h * dv:(h + 1) * dv] = og[:, h]
        return 0

    for c in range(tb // chunk):
        do_chunk(c, 0)

    @pl.when(t == pl.num_programs(1) - 1)
    def _():
        sfin_ref[...] = s_scr[...]


def _gdn(qkv, z, ab, conv_buf, s0, conv_w, a_log_row, dt_bias_row, gn_w, layer, batch, seq, chunk_pref,
         bb):
    n, dconv = qkv.shape
    _, heads, dk, dv = s0.shape
    assert dk == LANES and dv == LANES
    cw = conv_w.shape[1]
    tb = _tile(seq, 256)
    chunk = _tile(tb, chunk_pref)
    nt = seq // tb
    dz = heads * dv
    assert batch % bb == 0
    row_map = lambda b, t: (b, t, 0)
    state_map = lambda b, t: (b, 0, 0, 0)
    og, s_new = pl.pallas_call(
        functools.partial(_gdn_kernel, bb=bb, tb=tb, chunk=chunk, heads=heads, dk=dk, dv=dv, cw=cw),
        grid=(batch // bb, nt),
        in_specs=[
            pl.BlockSpec((None, 1, LANES), lambda b, t: (layer, 0, 0)),
            pl.BlockSpec((None, 1, LANES), lambda b, t: (layer, 0, 0)),
            pl.BlockSpec((bb, tb, dconv), row_map),
            pl.BlockSpec((bb, tb, dz), row_map),
            pl.BlockSpec((bb, tb, LANES), row_map),
            pl.BlockSpec((bb, cw - 1, dconv), lambda b, t: (b, 0, 0)),
            pl.BlockSpec((bb, heads, dk, dv), state_map),
            pl.BlockSpec((None, cw, dconv), lambda b, t: (layer, 0, 0)),
            pl.BlockSpec((None, 1, dv), lambda b, t: (layer, 0, 0)),
        ],
        out_specs=[
            pl.BlockSpec((bb, tb, dz), row_map),
            pl.BlockSpec((bb, heads, dk, dv), state_map),
        ],
        out_shape=[
            jax.ShapeDtypeStruct((batch, seq, dz), F32),
            jax.ShapeDtypeStruct((batch, heads, dk, dv), F32),
        ],
        scratch_shapes=[pltpu.VMEM((bb, CONV_PAD + tb, dconv), F32)]
        + [pltpu.VMEM((bb, heads, tb, LANES), F32)] * 6
        + [pltpu.VMEM((bb, heads, dk, dv), F32)],
        compiler_params=_cparams("parallel", "arbitrary"),
        name="gdn",
    )(a_log_row, dt_bias_row, qkv.reshape(batch, seq, dconv), z.reshape(batch, seq, dz),
      ab.reshape(batch, seq, LANES), conv_buf, s0, conv_w, gn_w)
    return og.reshape(n, dz), s_new


def _out_proj_stage(x_ref, y_ref, w_ref):
    return x_ref[...] + _dot(y_ref[...].astype(BF16), w_ref[...])


def _pre_out_proj(x, y, w, layer):
    n, d = x.shape
    tm = _tile(n, FFN_ROWS)
    specs = [
        pl.BlockSpec((tm, d), lambda i: (i, 0)),
        pl.BlockSpec((tm, y.shape[1]), lambda i: (i, 0)),
        _resident((None, y.shape[1], d), lambda i: (layer, 0, 0)),
    ]
    return ([x, y, w], specs, _out_proj_stage, [])


def _trunk(x, batch, seq, pos0, cache, pool_bufs, conv_bufs, rec_states, w):
    depth = w["norm_mix"].shape[0]
    heads_a, dh = w["heads_a"], w["dh"]
    da = heads_a * dh
    new_k, new_v, new_pool, new_conv, new_rec = [], [], [], [], []
    for layer in range(depth):
        i = layer // 2
        n, d = x.shape
        x = _ffn(_pre_plain(x), n, d, w["norm_ffn"], w["w_ffn_in"], w["w_ffn_out"], layer, 0)
        if layer % 2 == 0:
            db_ = w["w_in_even"].shape[2] - 3 * da
            if cache is None:
                q, u, kt, vt, kt_bf, vt_bf = _proj_even_prompt(
                    x, w["norm_mix"], layer, w["w_in_even"], w["w_kv_even_t"], i, batch, seq, da)
                oa = _attn_prompt(q, kt_bf, vt_bf, w["sb_bias"][i][None], batch, seq, dh)
                mixed = _pre_even_prompt(x, oa, u, w["w_pool"], w["pool_scale"], w["w_out_even"], i, seq)
                new_pool.append(u.reshape(batch, seq, db_)[:, seq - (POOL_HALO - 1):])
                new_k.append(jnp.transpose(kt.reshape(batch, heads_a, dh, seq), (0, 3, 1, 2)))
                new_v.append(jnp.transpose(vt.reshape(batch, heads_a, dh, seq), (0, 3, 1, 2)))
            else:
                segs = [(0, da, (0,)), (da, 2 * da, (1,)), (2 * da, 3 * da, (2,)),
                        (3 * da, 3 * da + db_, (3,))]
                q, k, v, u = _proj(x, (w["norm_mix"], layer), (w["w_in_even"], i), segs, [F32] * 4,
                                   [da, da, da, db_], "in_even")
                new_k.append(k.reshape(batch, seq, heads_a, dh))
                new_v.append(v.reshape(batch, seq, heads_a, dh))
                cache_k, cache_v, page_table = cache
                oa = _attn_sample(q, k, v, cache_k, cache_v, i, page_table, w["sb_bias"][i][None], seq)
                u3 = u.reshape(batch, seq, db_)
                buf = pool_bufs[i]
                padded = jnp.concatenate(
                    [jnp.zeros((batch, POOL_HALO - buf.shape[1], db_), F32), buf, u3], axis=1)
                mixed = _pre_even_sample(x, oa, padded, w["w_pool"], w["pool_scale"], w["w_out_even"], i,
                                         seq, pos0)
                new_pool.append(padded[:, padded.shape[1] - buf.shape[1]:])
        else:
            heads_c, dk, dv = rec_states.shape[2:]
            dconv = heads_c * (2 * dk + dv)
            dz = heads_c * dv
            segs = [(0, dconv, (0,)), (dconv, dconv + dz, (1,)), (dconv + dz, dconv + dz + LANES, (2,))]
            qkv, z, ab = _proj(x, (w["norm_mix"], layer), (w["w_in_odd"], i), segs, [F32] * 3,
                               [dconv, dz, LANES], "in_odd")
            og, s_new = _gdn(qkv, z, ab, conv_bufs[i], rec_states[i], w["conv_w"], w["a_log"],
                             w["dt_bias"], w["gn_w"], i, batch, seq, 64, 1 if cache is None else 4)
            mixed = _pre_out_proj(x, og, w["w_out_odd"], i)
            tail = conv_bufs.shape[2]
            xp = jnp.concatenate([conv_bufs[i], qkv.reshape(batch, seq, dconv)], axis=1)
            new_conv.append(xp[:, xp.shape[1] - tail:])
            new_rec.append(s_new)
        final_g = w["norm_final"] if layer == depth - 1 else None
        x = _ffn(mixed, n, d, w["norm_ffn"], w["w_ffn_in"], w["w_ffn_out"], layer, 1, final_g)
    return x, jnp.stack(new_k), jnp.stack(new_v), jnp.stack(new_pool), jnp.stack(new_conv), jnp.stack(new_rec)


def kernel(x_prompt, x_sample, cache_k, cache_v, page_table, state_pool, state_conv, state_rec, norm_ffn,
           w_ffn_in, w_ffn_out, norm_mix, w_in_even, sb_bias, w_pool, pool_scale, w_out_even, w_in_odd,
           conv_w, a_log, dt_bias, gn_w, w_out_odd, norm_final):
    bp, seq, d = x_prompt.shape
    db, dseq, _ = x_sample.shape
    n_even, n_pool, page, heads_a, dh = cache_k.shape
    n_odd, _, heads_c, dk, dv = state_rec.shape
    da = heads_a * dh
    dconv = heads_c * (2 * dk + dv)
    dz = heads_c * dv
    depth = norm_mix.shape[0]

    def lane_row(p):
        return jnp.pad(p.astype(F32), ((0, 0), (0, LANES - p.shape[1])))[:, None, :]

    w_odd = jnp.concatenate(
        [w_in_odd, jnp.zeros((n_odd, d, LANES - 2 * heads_c), w_in_odd.dtype)], axis=2).astype(BF16)
    w = dict(
        heads_a=heads_a, dh=dh,
        norm_ffn=norm_ffn.reshape(depth, 2, 1, d),
        w_ffn_in=w_ffn_in.astype(BF16),
        w_ffn_out=w_ffn_out.astype(BF16),
        norm_mix=norm_mix.reshape(depth, 1, d),
        w_in_even=w_in_even.astype(BF16),
        w_kv_even_t=jnp.transpose(w_in_even[:, :, da:3 * da], (0, 2, 1)).astype(BF16),
        sb_bias=sb_bias.astype(F32),
        w_pool=w_pool.astype(BF16),
        pool_scale=pool_scale.reshape(n_even, 1, -1),
        w_out_even=w_out_even.astype(BF16),
        w_in_odd=w_odd,
        conv_w=conv_w,
        a_log=lane_row(a_log),
        dt_bias=lane_row(dt_bias),
        gn_w=gn_w.reshape(n_odd, 1, dv),
        w_out_odd=w_out_odd.astype(BF16),
        norm_final=norm_final.reshape(1, d),
    )
    past_len = page_table.shape[1] * page

    yp, kp, vp, poolp, convp, recp = _trunk(
        x_prompt.reshape(bp * seq, d), bp, seq, 0, None,
        None,
        jnp.zeros((n_odd, bp, conv_w.shape[1] - 1, dconv), state_conv.dtype),
        jnp.zeros((n_odd, bp, heads_c, dk, dv), state_rec.dtype), w)
    ys, ks, vs, pools, convs, recs = _trunk(
        x_sample.reshape(db * dseq, d), db, dseq, past_len, (cache_k, cache_v, page_table),
        state_pool, state_conv, state_rec, w)
    return (yp.reshape(bp, seq, d), ys.reshape(db, dseq, d), kp, vp, ks, vs, poolp, pools, convp, convs,
            recp, recs)
```

```python
import functools

import jax
import jax.numpy as jnp
from jax import lax
from jax.experimental import pallas as pl
from jax.experimental.pallas import tpu as pltpu

F32 = jnp.float32
BF16 = jnp.bfloat16
EPS = 1e-6
FFN_HALF = 0.5
POOL_WINDOWS = (2, 4, 8, 16)
POOL_HALO = 16
LANES = 128
SUBLANES = 8
VMEM_LIMIT = 56 * 1024 * 1024
NEG_LOG2E = -1.4426950408889634


def _cparams(*sem):
    return pltpu.CompilerParams(dimension_semantics=sem, vmem_limit_bytes=VMEM_LIMIT)


def _tile(n, pref):
    t = pref
    while t > SUBLANES and n % t:
        t //= 2
    assert n % t == 0, (n, pref)
    return t


def _dot(a, b):
    return jnp.dot(a, b, preferred_element_type=F32)


def _dot_nt(a, b):
    return lax.dot_general(a, b, (((1,), (1,)), ((), ())), preferred_element_type=F32)


def _rms(x, g):
    return x * lax.rsqrt(jnp.mean(x * x, axis=-1, keepdims=True) + EPS) * g


def _softplus(x):
    return jnp.maximum(x, 0.0) + jnp.log(1.0 + jnp.exp2(jnp.abs(x) * NEG_LOG2E))


def _silu(x):
    return x * jax.nn.sigmoid(x)


def _iota2(shape, dim):
    return lax.broadcasted_iota(jnp.int32, shape, dim)


def _resident(shape, index_map):
    return pl.BlockSpec(shape, index_map, pipeline_mode=pl.Buffered(1))


def _ffn_kernel(*refs, n_pre, n_scratch, pre_fn, d_ff, ck, final):
    pre_refs = refs[:n_pre]
    scratch = refs[len(refs) - n_scratch:]
    g_ref, wi_ref, wo_ref, *rest = refs[n_pre:len(refs) - n_scratch]
    o_ref = rest[-1]
    x = pre_fn(*pre_refs, *scratch)
    h = _rms(x, g_ref[...]).astype(BF16)
    acc = None
    for c in range(d_ff // ck):
        gate = _dot(h, wi_ref[:, c * ck:(c + 1) * ck])
        up = _dot(h, wi_ref[:, d_ff + c * ck:d_ff + (c + 1) * ck])
        act = (_silu(gate) * up).astype(BF16)
        part = _dot(act, wo_ref[c * ck:(c + 1) * ck, :])
        acc = part if acc is None else acc + part
    y = x + FFN_HALF * acc
    if final:
        y = _rms(y, rest[0][...])
    o_ref[...] = y


FFN_ROWS = 512


def _ffn(pre, n, d, norm_ffn, w_in, w_out, layer, j, final_g=None):
    pre_args, pre_specs, pre_fn, scratch_shapes = pre
    d_ff = w_out.shape[2]
    tm = _tile(n, FFN_ROWS)
    ck = 256 if d_ff % 256 == 0 else d_ff
    in_specs = list(pre_specs) + [
        pl.BlockSpec((None, None, 1, d), lambda i: (layer, j, 0, 0)),
        _resident((None, None, d, 2 * d_ff), lambda i: (layer, j, 0, 0)),
        _resident((None, None, d_ff, d), lambda i: (layer, j, 0, 0)),
    ]
    args = list(pre_args) + [norm_ffn, w_in, w_out]
    if final_g is not None:
        in_specs.append(pl.BlockSpec((1, d), lambda i: (0, 0)))
        args.append(final_g)
    return pl.pallas_call(
        functools.partial(_ffn_kernel, n_pre=len(pre_args), n_scratch=len(scratch_shapes), pre_fn=pre_fn,
                          d_ff=d_ff, ck=ck, final=final_g is not None),
        grid=(n // tm,),
        in_specs=in_specs,
        out_specs=pl.BlockSpec((tm, d), lambda i: (i, 0)),
        out_shape=jax.ShapeDtypeStruct((n, d), F32),
        scratch_shapes=list(scratch_shapes),
        compiler_params=_cparams("parallel"),
        name="ffn",
    )(*args)


def _pre_plain(x):
    n, d = x.shape
    tm = _tile(n, FFN_ROWS)
    return ([x], [pl.BlockSpec((tm, d), lambda i: (i, 0))], lambda x_ref: x_ref[...], [])


def _proj_kernel(x_ref, g_ref, w_ref, *o_refs, segs):
    h = _rms(x_ref[...], g_ref[...]).astype(BF16)
    for (a, b, outs) in segs:
        y = _dot(h, w_ref[:, a:b])
        for oi in outs:
            o_refs[oi][...] = y.astype(o_refs[oi].dtype)


def _proj(x, g, w, segs, out_dtypes, out_widths, name):
    n, d = x.shape
    tm = _tile(n, 512)
    g_arr, g_idx = g
    w_arr, w_idx = w
    in_specs = [
        pl.BlockSpec((tm, d), lambda i: (i, 0)),
        pl.BlockSpec((None, 1, d), lambda i: (g_idx, 0, 0)),
        _resident((None, d, w_arr.shape[2]), lambda i: (w_idx, 0, 0)),
    ]
    return pl.pallas_call(
        functools.partial(_proj_kernel, segs=segs),
        grid=(n // tm,),
        in_specs=in_specs,
        out_specs=[pl.BlockSpec((tm, wd), lambda i: (i, 0)) for wd in out_widths],
        out_shape=[jax.ShapeDtypeStruct((n, wd), dt) for wd, dt in zip(out_widths, out_dtypes)],
        compiler_params=_cparams("parallel"),
        name=name,
    )(x, g_arr, w_arr)


def _proj_even_prompt_kernel(x_ref, g_ref, w_ref, wkv_t_ref, q_ref, u_ref, kt_ref, vt_ref, ktb_ref, vtb_ref,
                             *, da):
    h = _rms(x_ref[...], g_ref[...]).astype(BF16)
    q_ref[...] = _dot(h, w_ref[:, 0:da])
    u_ref[...] = _dot(h, w_ref[:, 3 * da:])
    kv_t = _dot_nt(wkv_t_ref[...], h)
    kt_ref[...] = kv_t[0:da]
    vt_ref[...] = kv_t[da:2 * da]
    ktb_ref[...] = kv_t[0:da].astype(BF16)
    vtb_ref[...] = kv_t[da:2 * da].astype(BF16)


def _proj_even_prompt(x, norm_mix, layer, w, wkv_t, i, batch, seq, da):
    n, d = x.shape
    tm = _tile(seq, 512)
    per_seq = seq // tm
    db_ = w.shape[2] - 3 * da
    row_spec = lambda wd: pl.BlockSpec((tm, wd), lambda t: (t, 0))
    lane_spec = pl.BlockSpec((None, da, tm), lambda t: (t // per_seq, 0, t % per_seq))
    return pl.pallas_call(
        functools.partial(_proj_even_prompt_kernel, da=da),
        grid=(n // tm,),
        in_specs=[
            row_spec(d),
            pl.BlockSpec((None, 1, d), lambda t: (layer, 0, 0)),
            _resident((None, d, w.shape[2]), lambda t: (i, 0, 0)),
            _resident((None, 2 * da, d), lambda t: (i, 0, 0)),
        ],
        out_specs=[row_spec(da), row_spec(db_), lane_spec, lane_spec, lane_spec, lane_spec],
        out_shape=[
            jax.ShapeDtypeStruct((n, da), F32),
            jax.ShapeDtypeStruct((n, db_), F32),
            jax.ShapeDtypeStruct((batch, da, seq), F32),
            jax.ShapeDtypeStruct((batch, da, seq), F32),
            jax.ShapeDtypeStruct((batch, da, seq), BF16),
            jax.ShapeDtypeStruct((batch, da, seq), BF16),
        ],
        compiler_params=_cparams("parallel"),
        name="in_even_prompt",
    )(x, norm_mix, w, wkv_t)


def _suffix_matrix(n, terms):
    row = _iota2((terms * n, n), 0) % n
    col = _iota2((terms * n, n), 1)
    return jnp.where(row > col, 1.0, 0.0).astype(BF16)


ATTN_LANE_BLOCKS = 2


def _attn_prompt_kernel(bias_ref, q_ref, k_ref, v_ref, o_ref, r_scr, acc_scr, *, tq, dh, scale, groups):
    hg = pl.program_id(1)
    i = pl.program_id(2)
    nh = LANES // dh
    lane = _iota2((tq, LANES), 1)
    heads = [(lane >= e * dh) & (lane < (e + 1) * dh) for e in range(nh)]
    u_tri = _suffix_matrix(tq, 1)
    causal = _iota2((nh * tq, tq), 1) < (_iota2((nh * tq, tq), 0) % tq)
    qs, biases = [], []
    for g in range(groups):
        q = q_ref[:, g * LANES:(g + 1) * LANES] * scale
        qs.append(jnp.concatenate([jnp.where(hm, q, 0.0) for hm in heads], axis=0).astype(BF16))
        biases.append([bias_ref[0, (hg * groups + g) * nh + e] for e in range(nh)])

    def tile(j, valid):
        r0 = pl.multiple_of(j * tq, tq)
        gs = range(groups)
        zs = [_dot(qs[g], k_ref[g * LANES:(g + 1) * LANES, pl.ds(r0, tq)]) for g in gs]
        zs = [jnp.concatenate([zs[g][e * tq:(e + 1) * tq] + biases[g][e] for e in range(nh)], axis=0)
              for g in gs]
        sps = [_softplus(z) for z in zs]
        logsig = [z - sp for z, sp in zip(zs, sps)]
        drops = sps if valid is None else [jnp.where(valid, sp, 0.0) for sp in sps]
        laters = [_dot(d.astype(BF16), u_tri) for d in drops]
        for g in gs:
            a = jnp.exp(logsig[g] - laters[g] - r_scr[g])
            if valid is not None:
                a = jnp.where(valid, a, 0.0)
            acc_scr[g] += _dot_nt(a.astype(BF16), v_ref[g * LANES:(g + 1) * LANES, pl.ds(r0, tq)])
            r_scr[g] += jnp.sum(drops[g], axis=1, keepdims=True)

    r_scr[...] = jnp.zeros_like(r_scr)
    acc_scr[...] = jnp.zeros_like(acc_scr)
    tile(i, causal)

    def body(s, c):
        tile(i - 1 - s, None)
        return c

    lax.fori_loop(0, i, body, 0)
    for g in range(groups):
        out = jnp.where(heads[0], acc_scr[g, 0:tq], 0.0)
        for e in range(1, nh):
            out = out + jnp.where(heads[e], acc_scr[g, e * tq:(e + 1) * tq], 0.0)
        o_ref[:, g * LANES:(g + 1) * LANES] = out


def _attn_prompt(q, k_bf, v_bf, sb_bias, batch, seq, dh):
    n, da = q.shape
    tq = _tile(seq, 256)
    nq = seq // tq
    groups = ATTN_LANE_BLOCKS if da % (ATTN_LANE_BLOCKS * LANES) == 0 else 1
    wl = groups * LANES
    rows = LANES // dh * tq
    return pl.pallas_call(
        functools.partial(_attn_prompt_kernel, tq=tq, dh=dh, scale=dh ** -0.5, groups=groups),
        grid=(batch, da // wl, nq),
        in_specs=[
            pl.BlockSpec(memory_space=pltpu.SMEM),
            pl.BlockSpec((tq, wl), lambda b, hg, i: (b * nq + i, hg)),
            pl.BlockSpec((None, wl, seq), lambda b, hg, i: (b, hg, 0)),
            pl.BlockSpec((None, wl, seq), lambda b, hg, i: (b, hg, 0)),
        ],
        out_specs=pl.BlockSpec((tq, wl), lambda b, hg, i: (b * nq + i, hg)),
        out_shape=jax.ShapeDtypeStruct((n, da), F32),
        scratch_shapes=[pltpu.VMEM((groups, rows, 1), F32), pltpu.VMEM((groups, rows, LANES), F32)],
        compiler_params=_cparams("parallel", "parallel", "arbitrary"),
        name="attn_prompt",
    )(sb_bias, q, k_bf, v_bf)


def _attn_sample_kernel(pt_ref, bias_ref, q_ref, kn_ref, vn_ref, *rest, nseq, n_pages, page, t_new, heads,
                        dh, scale):
    del pt_ref
    k_pages = [rest[s * n_pages:(s + 1) * n_pages] for s in range(nseq)]
    v_pages = [rest[(nseq + s) * n_pages:(nseq + s + 1) * n_pages] for s in range(nseq)]
    o_ref, r_scr, acc_scr = rest[2 * nseq * n_pages:]
    j = pl.program_id(1)
    ht = heads * t_new
    seqs = range(nseq)
    hs = range(heads)
    q = q_ref[...] * scale
    qh = [[q[s * t_new:(s + 1) * t_new, h * dh:(h + 1) * dh].astype(BF16) for h in hs] for s in seqs]
    u_tri = _suffix_matrix(page, 2)

    def tile(logits, weighted, valid, slots):
        zs = [jnp.concatenate([logits(s, h) + bias_ref[0, h] for h in hs], axis=0) for s in seqs]
        sps = [_softplus(z) for z in zs]
        logsig = [z - sp for z, sp in zip(zs, sps)]
        drops = sps if valid is None else [jnp.where(valid, sp, 0.0) for sp in sps]
        his = [d.astype(BF16) for d in drops]
        los = [(d - hi.astype(F32)).astype(BF16) for d, hi in zip(drops, his)]
        cols = [slice(p * page, (p + 1) * page) for p in range(slots)]
        later = _dot(jnp.concatenate(
            [jnp.concatenate([his[s][:, c], los[s][:, c]], axis=1) for s in seqs for c in cols], axis=0),
            u_tri)
        for s in seqs:
            r_run = r_scr[s]
            shift = []
            for p, c in enumerate(cols):
                r0 = (s * slots + p) * ht
                shift.append(later[r0:r0 + ht] + r_run)
                r_run = r_run + jnp.sum(drops[s][:, c], axis=1, keepdims=True)
            r_scr[s] = r_run
            a = jnp.exp(logsig[s] - jnp.concatenate(shift, axis=1))
            if valid is not None:
                a = jnp.where(valid, a, 0.0)
            acc_scr[s] += jnp.concatenate(
                [weighted(s, h, a[h * t_new:(h + 1) * t_new].astype(BF16)) for h in hs], axis=0)

    @pl.when(j == 0)
    def _():
        r_scr[...] = jnp.zeros_like(r_scr)
        acc_scr[...] = jnp.zeros_like(acc_scr)
        pad = jnp.zeros((page - t_new, dh), F32)
        kn = kn_ref[...]
        vn = vn_ref[...]
        valid_new = _iota2((ht, page), 1) < (_iota2((ht, page), 0) % t_new)

        def padded(x, s, h):
            return jnp.concatenate([x[s * t_new:(s + 1) * t_new, h * dh:(h + 1) * dh], pad],
                                   axis=0).astype(BF16)

        tile(lambda s, h: _dot_nt(qh[s][h], padded(kn, s, h)), lambda s, h, a: _dot(a, padded(vn, s, h)),
             valid_new, 1)

    def lanes_of(refs, h):
        return jnp.concatenate([r[h] for r in refs], axis=1).astype(BF16)

    tile(lambda s, h: _dot(qh[s][h], lanes_of(k_pages[s], h)),
         lambda s, h, a: _dot_nt(a, lanes_of(v_pages[s], h)), None, n_pages)

    @pl.when(j == pl.num_programs(1) - 1)
    def _():
        for s in seqs:
            o_ref[s * t_new:(s + 1) * t_new, :] = jnp.concatenate(
                [acc_scr[s, h * t_new:(h + 1) * t_new, :] for h in hs], axis=1)


SAMPLE_PAGES_PER_STEP = 16
SAMPLE_SEQS_PER_STEP = 2


def _attn_sample(q, k_new, v_new, cache_k, cache_v, layer, page_table, sb_bias, t_new):
    n, da = q.shape
    db, n_pages = page_table.shape
    _, _, page, heads, dh = cache_k.shape
    cache_k = jnp.transpose(cache_k, (0, 1, 3, 4, 2))
    cache_v = jnp.transpose(cache_v, (0, 1, 3, 4, 2))
    pps = SAMPLE_PAGES_PER_STEP if n_pages % SAMPLE_PAGES_PER_STEP == 0 else n_pages
    steps = n_pages // pps
    nseq = SAMPLE_SEQS_PER_STEP if db % SAMPLE_SEQS_PER_STEP == 0 else 1
    page_specs = [
        pl.BlockSpec((None, None, heads, dh, page),
                     lambda b, j, pt, s=s, p=p: (layer, pt[b * nseq + s, n_pages - 1 - (j * pps + p)],
                                                 0, 0, 0))
        for s in range(nseq) for p in range(pps)
    ]
    row_spec = pl.BlockSpec((nseq * t_new, da), lambda b, j, pt: (b, 0))
    grid_spec = pltpu.PrefetchScalarGridSpec(
        num_scalar_prefetch=1,
        grid=(db // nseq, steps),
        in_specs=[pl.BlockSpec(memory_space=pltpu.SMEM), row_spec, row_spec, row_spec]
        + page_specs + page_specs,
        out_specs=row_spec,
        scratch_shapes=[pltpu.VMEM((nseq, heads * t_new, 1), F32),
                        pltpu.VMEM((nseq, heads * t_new, dh), F32)],
    )
    n_blocks = nseq * pps
    return pl.pallas_call(
        functools.partial(_attn_sample_kernel, nseq=nseq, n_pages=pps, page=page, t_new=t_new, heads=heads,
                          dh=dh, scale=dh ** -0.5),
        grid_spec=grid_spec,
        out_shape=jax.ShapeDtypeStruct((n, da), F32),
        compiler_params=_cparams("parallel", "arbitrary"),
        name="attn_sample",
    )(page_table, sb_bias, q, k_new, v_new, *([cache_k] * n_blocks), *([cache_v] * n_blocks))


def _even_out_tail(x, oa, diffs, wp_ref, ps_ref, wo_ref):
    mixed = [oa.astype(BF16)]
    for g, diff in enumerate(diffs):
        cb = diff.shape[1]
        ob = _dot(diff.astype(BF16), wp_ref[g]) * ps_ref[:, g * cb:(g + 1) * cb]
        mixed.append(ob.astype(BF16))
    return x + _dot(jnp.concatenate(mixed, axis=1), wo_ref[...])


def _even_out_prompt_stage(x_ref, oa_ref, u_ref, halo_ref, wp_ref, ps_ref, wo_ref, pad_scr, *,
                           tp, tiles_per_seq, cb):
    it = pl.program_id(0) % tiles_per_seq
    pad_scr[0:POOL_HALO, :] = jnp.where(it == 0, 0.0, halo_ref[...])
    pad_scr[POOL_HALO:POOL_HALO + tp, :] = u_ref[...]
    pos = it * tp + _iota2((tp, 1), 0)
    diffs = []
    for g, w in enumerate(POOL_WINDOWS):
        cols = slice(g * cb, (g + 1) * cb)
        s = pad_scr[POOL_HALO:POOL_HALO + tp, cols]
        for j in range(1, w):
            s = s + pad_scr[POOL_HALO - j:POOL_HALO - j + tp, cols]
        cnt = jnp.minimum(w, pos + 1).astype(F32)
        diffs.append(s / cnt - u_ref[:, cols])
    return _even_out_tail(x_ref[...], oa_ref[...], diffs, wp_ref, ps_ref, wo_ref)


def _pre_even_prompt(x, oa, u, w_pool, pool_scale, w_out, layer, seq):
    n, d = x.shape
    db_ = u.shape[1]
    tp = _tile(n, FFN_ROWS)
    assert seq % tp == 0 and tp % POOL_HALO == 0
    tiles_per_seq = seq // tp
    hb = tp // POOL_HALO
    g_b = w_pool.shape[1]
    cb = db_ // g_b
    specs = [
        pl.BlockSpec((tp, d), lambda i: (i, 0)),
        pl.BlockSpec((tp, oa.shape[1]), lambda i: (i, 0)),
        pl.BlockSpec((tp, db_), lambda i: (i, 0)),
        pl.BlockSpec((POOL_HALO, db_), lambda i: (jnp.maximum(i * hb - 1, 0), 0)),
        _resident((None, g_b, cb, cb), lambda i: (layer, 0, 0, 0)),
        pl.BlockSpec((None, 1, db_), lambda i: (layer, 0, 0)),
        _resident((None, d, d), lambda i: (layer, 0, 0)),
    ]
    fn = functools.partial(_even_out_prompt_stage, tp=tp, tiles_per_seq=tiles_per_seq, cb=cb)
    return ([x, oa, u, u, w_pool, pool_scale, w_out], specs, fn, [pltpu.VMEM((POOL_HALO + tp, db_), F32)])


def _even_out_sample_stage(x_ref, oa_ref, p_ref, wp_ref, ps_ref, wo_ref, *, bb, t_new, pos0, cb):
    pos = pos0 + _iota2((1, t_new, 1), 1)
    diffs = []
    for g, w in enumerate(POOL_WINDOWS):
        cols = slice(g * cb, (g + 1) * cb)
        u3 = p_ref[:, POOL_HALO:POOL_HALO + t_new, cols]
        s = u3
        for j in range(1, w):
            s = s + p_ref[:, POOL_HALO - j:POOL_HALO - j + t_new, cols]
        cnt = jnp.minimum(w, pos + 1).astype(F32)
        diffs.append((s / cnt - u3).reshape(bb * t_new, cb))
    return _even_out_tail(x_ref[...], oa_ref[...], diffs, wp_ref, ps_ref, wo_ref)


def _pre_even_sample(x, oa, padded, w_pool, pool_scale, w_out, layer, t_new, pos0):
    n, d = x.shape
    nb, rows, db_ = padded.shape
    tm = _tile(n, FFN_ROWS)
    assert tm % t_new == 0
    bb = tm // t_new
    g_b = w_pool.shape[1]
    cb = db_ // g_b
    specs = [
        pl.BlockSpec((tm, d), lambda i: (i, 0)),
        pl.BlockSpec((tm, oa.shape[1]), lambda i: (i, 0)),
        pl.BlockSpec((bb, rows, db_), lambda i: (i, 0, 0)),
        _resident((None, g_b, cb, cb), lambda i: (layer, 0, 0, 0)),
        pl.BlockSpec((None, 1, db_), lambda i: (layer, 0, 0)),
        _resident((None, d, d), lambda i: (layer, 0, 0)),
    ]
    fn = functools.partial(_even_out_sample_stage, bb=bb, t_new=t_new, pos0=pos0, cb=cb)
    return ([x, oa, padded, w_pool, pool_scale, w_out], specs, fn, [])


CONV_PAD = SUBLANES


def _bdot(a, b):
    return lax.dot_general(a, b, (((2,), (1,)), ((0,), (0,))), preferred_element_type=F32)


def _bdot_nt(a, b):
    return lax.dot_general(a, b, (((2,), (2,)), ((0,), (0,))), preferred_element_type=F32)


def _bdot_tn(a, b):
    return lax.dot_general(a, b, (((1,), (1,)), ((0,), (0,))), preferred_element_type=F32)


def _unit_lower_inverse(m, eye):
    n = m.shape[-1]
    p = eye - m
    x = m
    k = 2
    while k < n:
        xb = x.astype(BF16)
        x = _bdot(xb, xb)
        p = p + _bdot(p.astype(BF16), x.astype(BF16))
        k *= 2
    return p


def _gdn_kernel(alog_ref, dtb_ref, qkv_ref, z_ref, ab_ref, cbuf_ref, s0_ref, cw_ref, gn_ref,
                o_ref, sfin_ref, xpad, q_scr, k_scr, v_scr, gc_scr, df_scr, be_scr, s_scr, *,
                bb, tb, chunk, heads, dk, dv, cw):
    t = pl.program_id(1)
    tail = cw - 1
    rows_all = bb * tb
    nb = bb * heads

    @pl.when(t == 0)
    def _():
        s_scr[...] = s0_ref[...]
        xpad[:, CONV_PAD - tail:CONV_PAD, :] = cbuf_ref[...]

    xpad[:, CONV_PAD:CONV_PAD + tb, :] = qkv_ref[...]

    def conv_silu(c0, width):
        acc = None
        for i in range(cw):
            r0 = CONV_PAD - tail + i
            term = xpad[:, r0:r0 + tb, c0:c0 + width] * cw_ref[i:i + 1, c0:c0 + width]
            acc = term if acc is None else acc + term
        return _silu(acc)

    def l2n(x):
        return x * lax.rsqrt(jnp.sum(x * x, axis=-1, keepdims=True) + EPS)

    for h in range(heads):
        q_scr[:, h] = l2n(conv_silu(h * dk, dk)) * (dk ** -0.5)
        k_scr[:, h] = l2n(conv_silu(heads * dk + h * dk, dk))
        v_scr[:, h] = conv_silu(2 * heads * dk + h * dv, dv)

    xpad[:, CONV_PAD - tail:CONV_PAD, :] = xpad[:, CONV_PAD + tb - tail:CONV_PAD + tb, :]

    ab = ab_ref[...].reshape(rows_all, LANES)
    g = -jnp.exp(alog_ref[...]) * _softplus(ab + dtb_ref[...])
    beta = jax.nn.sigmoid(ab)
    rr = _iota2((rows_all, rows_all), 0)
    cc = _iota2((rows_all, rows_all), 1)
    l_blk = jnp.where((rr // chunk == cc // chunk) & (rr >= cc), 1.0, 0.0).astype(BF16)
    after = (_iota2((rows_all, LANES), 0) % chunk) > _iota2((rows_all, LANES), 1)
    for h in range(heads):
        g_rep = jnp.broadcast_to(g[:, h:h + 1], (rows_all, LANES))
        rhs = jnp.concatenate([g_rep, jnp.where(after, g_rep, 0.0)], axis=1)
        hi = rhs.astype(BF16)
        lo = (rhs - hi.astype(F32)).astype(BF16)
        res = _dot(l_blk, hi) + _dot(l_blk, lo)
        gc_scr[:, h] = res[:, 0:LANES].reshape(bb, tb, LANES)
        df_scr[:, h] = res[:, LANES:2 * LANES].reshape(bb, tb, LANES)
        be_scr[:, h] = jnp.broadcast_to(beta[:, heads + h:heads + h + 1], (rows_all, LANES)).reshape(
            bb, tb, LANES)

    row = _iota2((chunk, chunk), 0)
    col = _iota2((chunk, chunk), 1)
    lower = row >= col
    strict = row > col
    eye = jnp.where(row == col, 1.0, 0.0)

    def do_chunk(c, _):
        r0 = c * chunk
        rows = pl.ds(r0, chunk)

        def ld(ref):
            return ref[:, :, rows, :].reshape(nb, chunk, LANES)

        q, k, v, gc, df, be = ld(q_scr), ld(k_scr), ld(v_scr), ld(gc_scr), ld(df_scr), ld(be_scr)
        kb = k * be
        decay = jnp.where(lower, jnp.exp(jnp.where(lower, df[:, :, 0:chunk], 0.0)), 0.0)
        kq = _bdot_nt(jnp.concatenate([kb, q], axis=1).astype(BF16), k.astype(BF16))
        m = jnp.where(strict, kq[:, 0:chunk] * decay, 0.0)
        attn = kq[:, chunk:2 * chunk] * decay
        t_inv = _unit_lower_inverse(m, eye)
        eg = jnp.exp(gc)
        sol = _bdot(t_inv.astype(BF16), jnp.concatenate([v * be, kb * eg], axis=2).astype(BF16))
        u_c = sol[:, :, 0:dv]
        w_c = sol[:, :, dv:dv + dk]
        gc_last = gc[:, chunk - 1:chunk, :]
        k_tail = k * jnp.exp(gc_last - gc)
        s = s_scr[...].reshape(nb, dk, dv)
        ws = _bdot(jnp.concatenate([w_c, q * eg], axis=1).astype(BF16), s.astype(BF16))
        v_new = u_c - ws[:, 0:chunk]
        v_new_b = v_new.astype(BF16)
        o = ws[:, chunk:2 * chunk] + _bdot(attn.astype(BF16), v_new_b)
        s_new = s * jnp.exp(gc_last) + _bdot_tn(k_tail.astype(BF16), v_new_b)
        s_scr[...] = s_new.reshape(bb, heads, dk, dv)
        zg = jnp.stack([z_ref[:, rows, h * dv:(h + 1) * dv] for h in range(heads)], axis=1)
        og = (_rms(o, gn_ref[...]) * _silu(zg.reshape(nb, chunk, dv))).reshape(bb, heads, chunk, dv)
        for h in range(heads):
            o_ref[:, rows, h * dv:(h + 1) * dv] = og[:, h]
        return 0

    for c in range(tb // chunk):
        do_chunk(c, 0)

    @pl.when(t == pl.num_programs(1) - 1)
    def _():
        sfin_ref[...] = s_scr[...]


def _gdn(qkv, z, ab, conv_buf, s0, conv_w, a_log_row, dt_bias_row, gn_w, layer, batch, seq, chunk_pref,
         bb):
    n, dconv = qkv.shape
    _, heads, dk, dv = s0.shape
    assert dk == LANES and dv == LANES
    cw = conv_w.shape[1]
    tb = _tile(seq, 256)
    chunk = _tile(tb, chunk_pref)
    nt = seq // tb
    dz = heads * dv
    assert batch % bb == 0
    row_map = lambda b, t: (b, t, 0)
    state_map = lambda b, t: (b, 0, 0, 0)
    og, s_new = pl.pallas_call(
        functools.partial(_gdn_kernel, bb=bb, tb=tb, chunk=chunk, heads=heads, dk=dk, dv=dv, cw=cw),
        grid=(batch // bb, nt),
        in_specs=[
            pl.BlockSpec((None, 1, LANES), lambda b, t: (layer, 0, 0)),
            pl.BlockSpec((None, 1, LANES), lambda b, t: (layer, 0, 0)),
            pl.BlockSpec((bb, tb, dconv), row_map),
            pl.BlockSpec((bb, tb, dz), row_map),
            pl.BlockSpec((bb, tb, LANES), row_map),
            pl.BlockSpec((bb, cw - 1, dconv), lambda b, t: (b, 0, 0)),
            pl.BlockSpec((bb, heads, dk, dv), state_map),
            pl.BlockSpec((None, cw, dconv), lambda b, t: (layer, 0, 0)),
            pl.BlockSpec((None, 1, dv), lambda b, t: (layer, 0, 0)),
        ],
        out_specs=[
            pl.BlockSpec((bb, tb, dz), row_map),
            pl.BlockSpec((bb, heads, dk, dv), state_map),
        ],
        out_shape=[
            jax.ShapeDtypeStruct((batch, seq, dz), F32),
            jax.ShapeDtypeStruct((batch, heads, dk, dv), F32),
        ],
        scratch_shapes=[pltpu.VMEM((bb, CONV_PAD + tb, dconv), F32)]
        + [pltpu.VMEM((bb, heads, tb, LANES), F32)] * 6
        + [pltpu.VMEM((bb, heads, dk, dv), F32)],
        compiler_params=_cparams("parallel", "arbitrary"),
        name="gdn",
    )(a_log_row, dt_bias_row, qkv.reshape(batch, seq, dconv), z.reshape(batch, seq, dz),
      ab.reshape(batch, seq, LANES), conv_buf, s0, conv_w, gn_w)
    return og.reshape(n, dz), s_new


def _out_proj_stage(x_ref, y_ref, w_ref):
    return x_ref[...] + _dot(y_ref[...].astype(BF16), w_ref[...])


def _pre_out_proj(x, y, w, layer):
    n, d = x.shape
    tm = _tile(n, FFN_ROWS)
    specs = [
        pl.BlockSpec((tm, d), lambda i: (i, 0)),
        pl.BlockSpec((tm, y.shape[1]), lambda i: (i, 0)),
        _resident((None, y.shape[1], d), lambda i: (layer, 0, 0)),
    ]
    return ([x, y, w], specs, _out_proj_stage, [])


def _trunk(x, batch, seq, pos0, cache, pool_bufs, conv_bufs, rec_states, w):
    depth = w["norm_mix"].shape[0]
    heads_a, dh = w["heads_a"], w["dh"]
    da = heads_a * dh
    new_k, new_v, new_pool, new_conv, new_rec = [], [], [], [], []
    for layer in range(depth):
        i = layer // 2
        n, d = x.shape
        x = _ffn(_pre_plain(x), n, d, w["norm_ffn"], w["w_ffn_in"], w["w_ffn_out"], layer, 0)
        if layer % 2 == 0:
            db_ = w["w_in_even"].shape[2] - 3 * da
            if cache is None:
                q, u, kt, vt, kt_bf, vt_bf = _proj_even_prompt(
                    x, w["norm_mix"], layer, w["w_in_even"], w["w_kv_even_t"], i, batch, seq, da)
                oa = _attn_prompt(q, kt_bf, vt_bf, w["sb_bias"][i][None], batch, seq, dh)
                mixed = _pre_even_prompt(x, oa, u, w["w_pool"], w["pool_scale"], w["w_out_even"], i, seq)
                new_pool.append(u.reshape(batch, seq, db_)[:, seq - (POOL_HALO - 1):])
                new_k.append(jnp.transpose(kt.reshape(batch, heads_a, dh, seq), (0, 3, 1, 2)))
                new_v.append(jnp.transpose(vt.reshape(batch, heads_a, dh, seq), (0, 3, 1, 2)))
            else:
                segs = [(0, da, (0,)), (da, 2 * da, (1,)), (2 * da, 3 * da, (2,)),
                        (3 * da, 3 * da + db_, (3,))]
                q, k, v, u = _proj(x, (w["norm_mix"], layer), (w["w_in_even"], i), segs, [F32] * 4,
                                   [da, da, da, db_], "in_even")
                new_k.append(k.reshape(batch, seq, heads_a, dh))
                new_v.append(v.reshape(batch, seq, heads_a, dh))
                cache_k, cache_v, page_table = cache
                oa = _attn_sample(q, k, v, cache_k, cache_v, i, page_table, w["sb_bias"][i][None], seq)
                u3 = u.reshape(batch, seq, db_)
                buf = pool_bufs[i]
                padded = jnp.concatenate(
                    [jnp.zeros((batch, POOL_HALO - buf.shape[1], db_), F32), buf, u3], axis=1)
                mixed = _pre_even_sample(x, oa, padded, w["w_pool"], w["pool_scale"], w["w_out_even"], i,
                                         seq, pos0)
                new_pool.append(padded[:, padded.shape[1] - buf.shape[1]:])
        else:
            heads_c, dk, dv = rec_states.shape[2:]
            dconv = heads_c * (2 * dk + dv)
            dz = heads_c * dv
            segs = [(0, dconv, (0,)), (dconv, dconv + dz, (1,)), (dconv + dz, dconv + dz + LANES, (2,))]
            qkv, z, ab = _proj(x, (w["norm_mix"], layer), (w["w_in_odd"], i), segs, [F32] * 3,
                               [dconv, dz, LANES], "in_odd")
            og, s_new = _gdn(qkv, z, ab, conv_bufs[i], rec_states[i], w["conv_w"], w["a_log"],
                             w["dt_bias"], w["gn_w"], i, batch, seq, 64, 1 if cache is None else 4)
            mixed = _pre_out_proj(x, og, w["w_out_odd"], i)
            tail = conv_bufs.shape[2]
            xp = jnp.concatenate([conv_bufs[i], qkv.reshape(batch, seq, dconv)], axis=1)
            new_conv.append(xp[:, xp.shape[1] - tail:])
            new_rec.append(s_new)
        final_g = w["norm_final"] if layer == depth - 1 else None
        x = _ffn(mixed, n, d, w["norm_ffn"], w["w_ffn_in"], w["w_ffn_out"], layer, 1, final_g)
    return x, jnp.stack(new_k), jnp.stack(new_v), jnp.stack(new_pool), jnp.stack(new_conv), jnp.stack(new_rec)


def kernel(x_prompt, x_sample, cache_k, cache_v, page_table, state_pool, state_conv, state_rec, norm_ffn,
           w_ffn_in, w_ffn_out, norm_mix, w_in_even, sb_bias, w_pool, pool_scale, w_out_even, w_in_odd,
           conv_w, a_log, dt_bias, gn_w, w_out_odd, norm_final):
    bp, seq, d = x_prompt.shape
    db, dseq, _ = x_sample.shape
    n_even, n_pool, page, heads_a, dh = cache_k.shape
    n_odd, _, heads_c, dk, dv = state_rec.shape
    da = heads_a * dh
    dconv = heads_c * (2 * dk + dv)
    dz = heads_c * dv
    depth = norm_mix.shape[0]

    def lane_row(p):
        return jnp.pad(p.astype(F32), ((0, 0), (0, LANES - p.shape[1])))[:, None, :]

    w_odd = jnp.concatenate(
        [w_in_odd, jnp.zeros((n_odd, d, LANES - 2 * heads_c), w_in_odd.dtype)], axis=2).astype(BF16)
    w = dict(
        heads_a=heads_a, dh=dh,
        norm_ffn=norm_ffn.reshape(depth, 2, 1, d),
        w_ffn_in=w_ffn_in.astype(BF16),
        w_ffn_out=w_ffn_out.astype(BF16),
        norm_mix=norm_mix.reshape(depth, 1, d),
        w_in_even=w_in_even.astype(BF16),
        w_kv_even_t=jnp.transpose(w_in_even[:, :, da:3 * da], (0, 2, 1)).astype(BF16),
        sb_bias=sb_bias.astype(F32),
        w_pool=w_pool.astype(BF16),
        pool_scale=pool_scale.reshape(n_even, 1, -1),
        w_out_even=w_out_even.astype(BF16),
        w_in_odd=w_odd,
        conv_w=conv_w,
        a_log=lane_row(a_log),
        dt_bias=lane_row(dt_bias),
        gn_w=gn_w.reshape(n_odd, 1, dv),
        w_out_odd=w_out_odd.astype(BF16),
        norm_final=norm_final.reshape(1, d),
    )
    past_len = page_table.shape[1] * page

    yp, kp, vp, poolp, convp, recp = _trunk(
        x_prompt.reshape(bp * seq, d), bp, seq, 0, None,
        None,
        jnp.zeros((n_odd, bp, conv_w.shape[1] - 1, dconv), state_conv.dtype),
        jnp.zeros((n_odd, bp, heads_c, dk, dv), state_rec.dtype), w)
    ys, ks, vs, pools, convs, recs = _trunk(
        x_sample.reshape(db * dseq, d), db, dseq, past_len, (cache_k, cache_v, page_table),
        state_pool, state_conv, state_rec, w)
    return (yp.reshape(bp, seq, d), ys.reshape(db, dseq, d), kp, vp, ks, vs, poolp, pools, convp, convs,
            recp, recs)
```

```python
import functools

import jax
import jax.numpy as jnp
from jax import lax
from jax.experimental import pallas as pl
from jax.experimental.pallas import tpu as pltpu

F32 = jnp.float32
BF16 = jnp.bfloat16
EPS = 1e-6
FFN_HALF = 0.5
POOL_WINDOWS = (2, 4, 8, 16)
POOL_HALO = 16
LANES = 128
SUBLANES = 8
VMEM_LIMIT = 56 * 1024 * 1024
NEG_LOG2E = -1.4426950408889634


def _cparams(*sem):
    return pltpu.CompilerParams(dimension_semantics=sem, vmem_limit_bytes=VMEM_LIMIT)


def _tile(n, pref):
    t = pref
    while t > SUBLANES and n % t:
        t //= 2
    assert n % t == 0, (n, pref)
    return t


def _dot(a, b):
    return jnp.dot(a, b, preferred_element_type=F32)


def _dot_nt(a, b):
    return lax.dot_general(a, b, (((1,), (1,)), ((), ())), preferred_element_type=F32)


def _rms(x, g):
    return x * lax.rsqrt(jnp.mean(x * x, axis=-1, keepdims=True) + EPS) * g


def _softplus(x):
    return jnp.maximum(x, 0.0) + jnp.log(1.0 + jnp.exp2(jnp.abs(x) * NEG_LOG2E))


def _silu(x):
    return x * jax.nn.sigmoid(x)


def _iota2(shape, dim):
    return lax.broadcasted_iota(jnp.int32, shape, dim)


def _resident(shape, index_map):
    return pl.BlockSpec(shape, index_map, pipeline_mode=pl.Buffered(1))


def _ffn_kernel(*refs, n_pre, n_scratch, pre_fn, d_ff, ck, final):
    pre_refs = refs[:n_pre]
    scratch = refs[len(refs) - n_scratch:]
    g_ref, wi_ref, wo_ref, *rest = refs[n_pre:len(refs) - n_scratch]
    o_ref = rest[-1]
    x = pre_fn(*pre_refs, *scratch)
    h = _rms(x, g_ref[...]).astype(BF16)
    acc = None
    for c in range(d_ff // ck):
        gate = _dot(h, wi_ref[:, c * ck:(c + 1) * ck])
        up = _dot(h, wi_ref[:, d_ff + c * ck:d_ff + (c + 1) * ck])
        act = (_silu(gate) * up).astype(BF16)
        part = _dot(act, wo_ref[c * ck:(c + 1) * ck, :])
        acc = part if acc is None else acc + part
    y = x + FFN_HALF * acc
    if final:
        y = _rms(y, rest[0][...])
    o_ref[...] = y


FFN_ROWS = 512


def _ffn(pre, n, d, norm_ffn, w_in, w_out, layer, j, final_g=None):
    pre_args, pre_specs, pre_fn, scratch_shapes = pre
    d_ff = w_out.shape[2]
    tm = _tile(n, FFN_ROWS)
    ck = 256 if d_ff % 256 == 0 else d_ff
    in_specs = list(pre_specs) + [
        pl.BlockSpec((None, None, 1, d), lambda i: (layer, j, 0, 0)),
        _resident((None, None, d, 2 * d_ff), lambda i: (layer, j, 0, 0)),
        _resident((None, None, d_ff, d), lambda i: (layer, j, 0, 0)),
    ]
    args = list(pre_args) + [norm_ffn, w_in, w_out]
    if final_g is not None:
        in_specs.append(pl.BlockSpec((1, d), lambda i: (0, 0)))
        args.append(final_g)
    return pl.pallas_call(
        functools.partial(_ffn_kernel, n_pre=len(pre_args), n_scratch=len(scratch_shapes), pre_fn=pre_fn,
                          d_ff=d_ff, ck=ck, final=final_g is not None),
        grid=(n // tm,),
        in_specs=in_specs,
        out_specs=pl.BlockSpec((tm, d), lambda i: (i, 0)),
        out_shape=jax.ShapeDtypeStruct((n, d), F32),
        scratch_shapes=list(scratch_shapes),
        compiler_params=_cparams("parallel"),
        name="ffn",
    )(*args)


def _pre_plain(x):
    n, d = x.shape
    tm = _tile(n, FFN_ROWS)
    return ([x], [pl.BlockSpec((tm, d), lambda i: (i, 0))], lambda x_ref: x_ref[...], [])


def _proj_kernel(x_ref, g_ref, w_ref, *o_refs, segs):
    h = _rms(x_ref[...], g_ref[...]).astype(BF16)
    for (a, b, outs) in segs:
        y = _dot(h, w_ref[:, a:b])
        for oi in outs:
            o_refs[oi][...] = y.astype(o_refs[oi].dtype)


def _proj(x, g, w, segs, out_dtypes, out_widths, name):
    n, d = x.shape
    tm = _tile(n, 512)
    g_arr, g_idx = g
    w_arr, w_idx = w
    in_specs = [
        pl.BlockSpec((tm, d), lambda i: (i, 0)),
        pl.BlockSpec((None, 1, d), lambda i: (g_idx, 0, 0)),
        _resident((None, d, w_arr.shape[2]), lambda i: (w_idx, 0, 0)),
    ]
    return pl.pallas_call(
        functools.partial(_proj_kernel, segs=segs),
        grid=(n // tm,),
        in_specs=in_specs,
        out_specs=[pl.BlockSpec((tm, wd), lambda i: (i, 0)) for wd in out_widths],
        out_shape=[jax.ShapeDtypeStruct((n, wd), dt) for wd, dt in zip(out_widths, out_dtypes)],
        compiler_params=_cparams("parallel"),
        name=name,
    )(x, g_arr, w_arr)


def _proj_even_prompt_kernel(x_ref, g_ref, w_ref, wkv_t_ref, q_ref, u_ref, kt_ref, vt_ref, ktb_ref, vtb_ref,
                             *, da):
    h = _rms(x_ref[...], g_ref[...]).astype(BF16)
    q_ref[...] = _dot(h, w_ref[:, 0:da])
    u_ref[...] = _dot(h, w_ref[:, 3 * da:])
    kv_t = _dot_nt(wkv_t_ref[...], h)
    kt_ref[...] = kv_t[0:da]
    vt_ref[...] = kv_t[da:2 * da]
    ktb_ref[...] = kv_t[0:da].astype(BF16)
    vtb_ref[...] = kv_t[da:2 * da].astype(BF16)


def _proj_even_prompt(x, norm_mix, layer, w, wkv_t, i, batch, seq, da):
    n, d = x.shape
    tm = _tile(seq, 512)
    per_seq = seq // tm
    db_ = w.shape[2] - 3 * da
    row_spec = lambda wd: pl.BlockSpec((tm, wd), lambda t: (t, 0))
    lane_spec = pl.BlockSpec((None, da, tm), lambda t: (t // per_seq, 0, t % per_seq))
    return pl.pallas_call(
        functools.partial(_proj_even_prompt_kernel, da=da),
        grid=(n // tm,),
        in_specs=[
            row_spec(d),
            pl.BlockSpec((None, 1, d), lambda t: (layer, 0, 0)),
            _resident((None, d, w.shape[2]), lambda t: (i, 0, 0)),
            _resident((None, 2 * da, d), lambda t: (i, 0, 0)),
        ],
        out_specs=[row_spec(da), row_spec(db_), lane_spec, lane_spec, lane_spec, lane_spec],
        out_shape=[
            jax.ShapeDtypeStruct((n, da), F32),
            jax.ShapeDtypeStruct((n, db_), F32),
            jax.ShapeDtypeStruct((batch, da, seq), F32),
            jax.ShapeDtypeStruct((batch, da, seq), F32),
            jax.ShapeDtypeStruct((batch, da, seq), BF16),
            jax.ShapeDtypeStruct((batch, da, seq), BF16),
        ],
        compiler_params=_cparams("parallel"),
        name="in_even_prompt",
    )(x, norm_mix, w, wkv_t)


def _suffix_matrix(n, terms):
    row = _iota2((terms * n, n), 0) % n
    col = _iota2((terms * n, n), 1)
    return jnp.where(row > col, 1.0, 0.0).astype(BF16)


ATTN_LANE_BLOCKS = 2


def _attn_prompt_kernel(bias_ref, q_ref, k_ref, v_ref, o_ref, r_scr, acc_scr, *, tq, dh, scale, groups):
    hg = pl.program_id(1)
    i = pl.program_id(2)
    nh = LANES // dh
    lane = _iota2((tq, LANES), 1)
    heads = [(lane >= e * dh) & (lane < (e + 1) * dh) for e in range(nh)]
    u_tri = _suffix_matrix(tq, 1)
    causal = _iota2((nh * tq, tq), 1) < (_iota2((nh * tq, tq), 0) % tq)
    qs, biases = [], []
    for g in range(groups):
        q = q_ref[:, g * LANES:(g + 1) * LANES] * scale
        qs.append(jnp.concatenate([jnp.where(hm, q, 0.0) for hm in heads], axis=0).astype(BF16))
        biases.append([bias_ref[0, (hg * groups + g) * nh + e] for e in range(nh)])

    def tile(j, valid):
        r0 = pl.multiple_of(j * tq, tq)
        gs = range(groups)
        zs = [_dot(qs[g], k_ref[g * LANES:(g + 1) * LANES, pl.ds(r0, tq)]) for g in gs]
        zs = [jnp.concatenate([zs[g][e * tq:(e + 1) * tq] + biases[g][e] for e in range(nh)], axis=0)
              for g in gs]
        sps = [_softplus(z) for z in zs]
        logsig = [z - sp for z, sp in zip(zs, sps)]
        drops = sps if valid is None else [jnp.where(valid, sp, 0.0) for sp in sps]
        laters = [_dot(d.astype(BF16), u_tri) for d in drops]
        for g in gs:
            a = jnp.exp(logsig[g] - laters[g] - r_scr[g])
            if valid is not None:
                a = jnp.where(valid, a, 0.0)
            acc_scr[g] += _dot_nt(a.astype(BF16), v_ref[g * LANES:(g + 1) * LANES, pl.ds(r0, tq)])
            r_scr[g] += jnp.sum(drops[g], axis=1, keepdims=True)

    r_scr[...] = jnp.zeros_like(r_scr)
    acc_scr[...] = jnp.zeros_like(acc_scr)
    tile(i, causal)

    def body(s, c):
        tile(i - 1 - s, None)
        return c

    lax.fori_loop(0, i, body, 0)
    for g in range(groups):
        out = jnp.where(heads[0], acc_scr[g, 0:tq], 0.0)
        for e in range(1, nh):
            out = out + jnp.where(heads[e], acc_scr[g, e * tq:(e + 1) * tq], 0.0)
        o_ref[:, g * LANES:(g + 1) * LANES] = out


def _attn_prompt(q, k_bf, v_bf, sb_bias, batch, seq, dh):
    n, da = q.shape
    tq = _tile(seq, 256)
    nq = seq // tq
    groups = ATTN_LANE_BLOCKS if da % (ATTN_LANE_BLOCKS * LANES) == 0 else 1
    wl = groups * LANES
    rows = LANES // dh * tq
    return pl.pallas_call(
        functools.partial(_attn_prompt_kernel, tq=tq, dh=dh, scale=dh ** -0.5, groups=groups),
        grid=(batch, da // wl, nq),
        in_specs=[
            pl.BlockSpec(memory_space=pltpu.SMEM),
            pl.BlockSpec((tq, wl), lambda b, hg, i: (b * nq + i, hg)),
            pl.BlockSpec((None, wl, seq), lambda b, hg, i: (b, hg, 0)),
            pl.BlockSpec((None, wl, seq), lambda b, hg, i: (b, hg, 0)),
        ],
        out_specs=pl.BlockSpec((tq, wl), lambda b, hg, i: (b * nq + i, hg)),
        out_shape=jax.ShapeDtypeStruct((n, da), F32),
        scratch_shapes=[pltpu.VMEM((groups, rows, 1), F32), pltpu.VMEM((groups, rows, LANES), F32)],
        compiler_params=_cparams("parallel", "parallel", "arbitrary"),
        name="attn_prompt",
    )(sb_bias, q, k_bf, v_bf)


def _attn_sample_kernel(pt_ref, bias_ref, q_ref, kn_ref, vn_ref, *rest, nseq, n_pages, page, t_new, heads,
                        dh, scale):
    del pt_ref
    k_pages = [rest[s * n_pages:(s + 1) * n_pages] for s in range(nseq)]
    v_pages = [rest[(nseq + s) * n_pages:(nseq + s + 1) * n_pages] for s in range(nseq)]
    o_ref, r_scr, acc_scr = rest[2 * nseq * n_pages:]
    j = pl.program_id(1)
    ht = heads * t_new
    seqs = range(nseq)
    hs = range(heads)
    q = q_ref[...] * scale
    qh = [[q[s * t_new:(s + 1) * t_new, h * dh:(h + 1) * dh].astype(BF16) for h in hs] for s in seqs]
    u_tri = _suffix_matrix(page, 2)

    def tile(logits, weighted, valid, slots):
        zs = [jnp.concatenate([logits(s, h) + bias_ref[0, h] for h in hs], axis=0) for s in seqs]
        sps = [_softplus(z) for z in zs]
        logsig = [z - sp for z, sp in zip(zs, sps)]
        drops = sps if valid is None else [jnp.where(valid, sp, 0.0) for sp in sps]
        his = [d.astype(BF16) for d in drops]
        los = [(d - hi.astype(F32)).astype(BF16) for d, hi in zip(drops, his)]
        cols = [slice(p * page, (p + 1) * page) for p in range(slots)]
        later = _dot(jnp.concatenate(
            [jnp.concatenate([his[s][:, c], los[s][:, c]], axis=1) for s in seqs for c in cols], axis=0),
            u_tri)
        for s in seqs:
            r_run = r_scr[s]
            shift = []
            for p, c in enumerate(cols):
                r0 = (s * slots + p) * ht
                shift.append(later[r0:r0 + ht] + r_run)
                r_run = r_run + jnp.sum(drops[s][:, c], axis=1, keepdims=True)
            r_scr[s] = r_run
            a = jnp.exp(logsig[s] - jnp.concatenate(shift, axis=1))
            if valid is not None:
                a = jnp.where(valid, a, 0.0)
            acc_scr[s] += jnp.concatenate(
                [weighted(s, h, a[h * t_new:(h + 1) * t_new].astype(BF16)) for h in hs], axis=0)

    @pl.when(j == 0)
    def _():
        r_scr[...] = jnp.zeros_like(r_scr)
        acc_scr[...] = jnp.zeros_like(acc_scr)
        pad = jnp.zeros((page - t_new, dh), F32)
        kn = kn_ref[...]
        vn = vn_ref[...]
        valid_new = _iota2((ht, page), 1) < (_iota2((ht, page), 0) % t_new)

        def padded(x, s, h):
            return jnp.concatenate([x[s * t_new:(s + 1) * t_new, h * dh:(h + 1) * dh], pad],
                                   axis=0).astype(BF16)

        tile(lambda s, h: _dot_nt(qh[s][h], padded(kn, s, h)), lambda s, h, a: _dot(a, padded(vn, s, h)),
             valid_new, 1)

    def lanes_of(refs, h):
        return jnp.concatenate([r[h] for r in refs], axis=1).astype(BF16)

    tile(lambda s, h: _dot(qh[s][h], lanes_of(k_pages[s], h)),
         lambda s, h, a: _dot_nt(a, lanes_of(v_pages[s], h)), None, n_pages)

    @pl.when(j == pl.num_programs(1) - 1)
    def _():
        for s in seqs:
            o_ref[s * t_new:(s + 1) * t_new, :] = jnp.concatenate(
                [acc_scr[s, h * t_new:(h + 1) * t_new, :] for h in hs], axis=1)


SAMPLE_PAGES_PER_STEP = 16
SAMPLE_SEQS_PER_STEP = 2


def _attn_sample(q, k_new, v_new, cache_k, cache_v, layer, page_table, sb_bias, t_new):
    n, da = q.shape
    db, n_pages = page_table.shape
    _, _, page, heads, dh = cache_k.shape
    cache_k = jnp.transpose(cache_k, (0, 1, 3, 4, 2))
    cache_v = jnp.transpose(cache_v, (0, 1, 3, 4, 2))
    pps = SAMPLE_PAGES_PER_STEP if n_pages % SAMPLE_PAGES_PER_STEP == 0 else n_pages
    steps = n_pages // pps
    nseq = SAMPLE_SEQS_PER_STEP if db % SAMPLE_SEQS_PER_STEP == 0 else 1
    page_specs = [
        pl.BlockSpec((None, None, heads, dh, page),
                     lambda b, j, pt, s=s, p=p: (layer, pt[b * nseq + s, n_pages - 1 - (j * pps + p)],
                                                 0, 0, 0))
        for s in range(nseq) for p in range(pps)
    ]
    row_spec = pl.BlockSpec((nseq * t_new, da), lambda b, j, pt: (b, 0))
    grid_spec = pltpu.PrefetchScalarGridSpec(
        num_scalar_prefetch=1,
        grid=(db // nseq, steps),
        in_specs=[pl.BlockSpec(memory_space=pltpu.SMEM), row_spec, row_spec, row_spec]
        + page_specs + page_specs,
        out_specs=row_spec,
        scratch_shapes=[pltpu.VMEM((nseq, heads * t_new, 1), F32),
                        pltpu.VMEM((nseq, heads * t_new, dh), F32)],
    )
    n_blocks = nseq * pps
    return pl.pallas_call(
        functools.partial(_attn_sample_kernel, nseq=nseq, n_pages=pps, page=page, t_new=t_new, heads=heads,
                          dh=dh, scale=dh ** -0.5),
        grid_spec=grid_spec,
        out_shape=jax.ShapeDtypeStruct((n, da), F32),
        compiler_params=_cparams("parallel", "arbitrary"),
        name="attn_sample",
    )(page_table, sb_bias, q, k_new, v_new, *([cache_k] * n_blocks), *([cache_v] * n_blocks))


def _even_out_tail(x, oa, diffs, wp_ref, ps_ref, wo_ref):
    mixed = [oa.astype(BF16)]
    for g, diff in enumerate(diffs):
        cb = diff.shape[1]
        ob = _dot(diff.astype(BF16), wp_ref[g]) * ps_ref[:, g * cb:(g + 1) * cb]
        mixed.append(ob.astype(BF16))
    return x + _dot(jnp.concatenate(mixed, axis=1), wo_ref[...])


def _even_out_prompt_stage(x_ref, oa_ref, u_ref, halo_ref, wp_ref, ps_ref, wo_ref, pad_scr, *,
                           tp, tiles_per_seq, cb):
    it = pl.program_id(0) % tiles_per_seq
    pad_scr[0:POOL_HALO, :] = jnp.where(it == 0, 0.0, halo_ref[...])
    pad_scr[POOL_HALO:POOL_HALO + tp, :] = u_ref[...]
    pos = it * tp + _iota2((tp, 1), 0)
    diffs = []
    for g, w in enumerate(POOL_WINDOWS):
        cols = slice(g * cb, (g + 1) * cb)
        s = pad_scr[POOL_HALO:POOL_HALO + tp, cols]
        for j in range(1, w):
            s = s + pad_scr[POOL_HALO - j:POOL_HALO - j + tp, cols]
        cnt = jnp.minimum(w, pos + 1).astype(F32)
        diffs.append(s / cnt - u_ref[:, cols])
    return _even_out_tail(x_ref[...], oa_ref[...], diffs, wp_ref, ps_ref, wo_ref)


def _pre_even_prompt(x, oa, u, w_pool, pool_scale, w_out, layer, seq):
    n, d = x.shape
    db_ = u.shape[1]
    tp = _tile(n, FFN_ROWS)
    assert seq % tp == 0 and tp % POOL_HALO == 0
    tiles_per_seq = seq // tp
    hb = tp // POOL_HALO
    g_b = w_pool.shape[1]
    cb = db_ // g_b
    specs = [
        pl.BlockSpec((tp, d), lambda i: (i, 0)),
        pl.BlockSpec((tp, oa.shape[1]), lambda i: (i, 0)),
        pl.BlockSpec((tp, db_), lambda i: (i, 0)),
        pl.BlockSpec((POOL_HALO, db_), lambda i: (jnp.maximum(i * hb - 1, 0), 0)),
        _resident((None, g_b, cb, cb), lambda i: (layer, 0, 0, 0)),
        pl.BlockSpec((None, 1, db_), lambda i: (layer, 0, 0)),
        _resident((None, d, d), lambda i: (layer, 0, 0)),
    ]
    fn = functools.partial(_even_out_prompt_stage, tp=tp, tiles_per_seq=tiles_per_seq, cb=cb)
    return ([x, oa, u, u, w_pool, pool_scale, w_out], specs, fn, [pltpu.VMEM((POOL_HALO + tp, db_), F32)])


def _even_out_sample_stage(x_ref, oa_ref, p_ref, wp_ref, ps_ref, wo_ref, *, bb, t_new, pos0, cb):
    pos = pos0 + _iota2((1, t_new, 1), 1)
    diffs = []
    for g, w in enumerate(POOL_WINDOWS):
        cols = slice(g * cb, (g + 1) * cb)
        u3 = p_ref[:, POOL_HALO:POOL_HALO + t_new, cols]
        s = u3
        for j in range(1, w):
            s = s + p_ref[:, POOL_HALO - j:POOL_HALO - j + t_new, cols]
        cnt = jnp.minimum(w, pos + 1).astype(F32)
        diffs.append((s / cnt - u3).reshape(bb * t_new, cb))
    return _even_out_tail(x_ref[...], oa_ref[...], diffs, wp_ref, ps_ref, wo_ref)


def _pre_even_sample(x, oa, padded, w_pool, pool_scale, w_out, layer, t_new, pos0):
    n, d = x.shape
    nb, rows, db_ = padded.shape
    tm = _tile(n, FFN_ROWS)
    assert tm % t_new == 0
    bb = tm // t_new
    g_b = w_pool.shape[1]
    cb = db_ // g_b
    specs = [
        pl.BlockSpec((tm, d), lambda i: (i, 0)),
        pl.BlockSpec((tm, oa.shape[1]), lambda i: (i, 0)),
        pl.BlockSpec((bb, rows, db_), lambda i: (i, 0, 0)),
        _resident((None, g_b, cb, cb), lambda i: (layer, 0, 0, 0)),
        pl.BlockSpec((None, 1, db_), lambda i: (layer, 0, 0)),
        _resident((None, d, d), lambda i: (layer, 0, 0)),
    ]
    fn = functools.partial(_even_out_sample_stage, bb=bb, t_new=t_new, pos0=pos0, cb=cb)
    return ([x, oa, padded, w_pool, pool_scale, w_out], specs, fn, [])


CONV_PAD = SUBLANES


def _bdot(a, b):
    return lax.dot_general(a, b, (((2,), (1,)), ((0,), (0,))), preferred_element_type=F32)


def _bdot_nt(a, b):
    return lax.dot_general(a, b, (((2,), (2,)), ((0,), (0,))), preferred_element_type=F32)


def _bdot_tn(a, b):
    return lax.dot_general(a, b, (((1,), (1,)), ((0,), (0,))), preferred_element_type=F32)


def _unit_lower_inverse(m, eye):
    n = m.shape[-1]
    p = eye - m
    x = m
    k = 2
    while k < n:
        xb = x.astype(BF16)
        x = _bdot(xb, xb)
        p = p + _bdot(p.astype(BF16), x.astype(BF16))
        k *= 2
    return p


def _gdn_kernel(alog_ref, dtb_ref, qkv_ref, z_ref, ab_ref, cbuf_ref, s0_ref, cw_ref, gn_ref,
                o_ref, sfin_ref, xpad, q_scr, k_scr, v_scr, gc_scr, df_scr, be_scr, s_scr, *,
                bb, tb, chunk, heads, dk, dv, cw, nt, n_tiles):
    s_id = pl.program_id(0)
    conv_t = jnp.minimum(s_id, n_tiles - 1) % nt
    scan_t = jnp.maximum(s_id - 1, 0) % nt
    w_slot = s_id % 2
    r_slot = (s_id + 1) % 2
    tail = cw - 1
    rows_all = bb * tb
    nb = bb * heads

    @pl.when(s_id == 0)
    def _():
        for ref in (q_scr, k_scr, v_scr, gc_scr, df_scr, be_scr):
            ref[1] = jnp.zeros(ref.shape[1:], F32)

    @pl.when(conv_t == 0)
    def _():
        xpad[:, CONV_PAD - tail:CONV_PAD, :] = cbuf_ref[...]

    @pl.when(scan_t == 0)
    def _():
        s_scr[...] = s0_ref[...]

    def conv_half():
        xpad[:, CONV_PAD:CONV_PAD + tb, :] = qkv_ref[...]

        def conv_silu(c0, width):
            acc = None
            for i in range(cw):
                r0 = CONV_PAD - tail + i
                term = xpad[:, r0:r0 + tb, c0:c0 + width] * cw_ref[i:i + 1, c0:c0 + width]
                acc = term if acc is None else acc + term
            return _silu(acc)

        def l2n(x):
            return x * lax.rsqrt(jnp.sum(x * x, axis=-1, keepdims=True) + EPS)

        for h in range(heads):
            q_scr[w_slot, :, h] = l2n(conv_silu(h * dk, dk)) * (dk ** -0.5)
            k_scr[w_slot, :, h] = l2n(conv_silu(heads * dk + h * dk, dk))
            v_scr[w_slot, :, h] = conv_silu(2 * heads * dk + h * dv, dv)

        xpad[:, CONV_PAD - tail:CONV_PAD, :] = xpad[:, CONV_PAD + tb - tail:CONV_PAD + tb, :]

        ab = ab_ref[...].reshape(rows_all, LANES)
        g = -jnp.exp(alog_ref[...]) * _softplus(ab + dtb_ref[...])
        beta = jax.nn.sigmoid(ab)
        rr = _iota2((rows_all, rows_all), 0)
        cc = _iota2((rows_all, rows_all), 1)
        l_blk = jnp.where((rr // chunk == cc // chunk) & (rr >= cc), 1.0, 0.0).astype(BF16)
        after = (_iota2((rows_all, LANES), 0) % chunk) > _iota2((rows_all, LANES), 1)
        for h in range(heads):
            g_rep = jnp.broadcast_to(g[:, h:h + 1], (rows_all, LANES))
            rhs = jnp.concatenate([g_rep, jnp.where(after, g_rep, 0.0)], axis=1)
            hi = rhs.astype(BF16)
            lo = (rhs - hi.astype(F32)).astype(BF16)
            res = _dot(l_blk, hi) + _dot(l_blk, lo)
            gc_scr[w_slot, :, h] = res[:, 0:LANES].reshape(bb, tb, LANES)
            df_scr[w_slot, :, h] = res[:, LANES:2 * LANES].reshape(bb, tb, LANES)
            be_scr[w_slot, :, h] = jnp.broadcast_to(
                beta[:, heads + h:heads + h + 1], (rows_all, LANES)).reshape(bb, tb, LANES)

    row = _iota2((chunk, chunk), 0)
    col = _iota2((chunk, chunk), 1)
    lower = row >= col
    strict = row > col
    eye = jnp.where(row == col, 1.0, 0.0)

    nc = tb // chunk

    def ld(ref):
        return ref[r_slot].reshape(nb * nc, chunk, LANES)

    q, k, v, gc, df, be = ld(q_scr), ld(k_scr), ld(v_scr), ld(gc_scr), ld(df_scr), ld(be_scr)
    kb = k * be
    decay = jnp.where(lower, jnp.exp(jnp.where(lower, df[:, :, 0:chunk], 0.0)), 0.0)
    kq = _bdot_nt(jnp.concatenate([kb, q], axis=1).astype(BF16), k.astype(BF16))
    m = jnp.where(strict, kq[:, 0:chunk] * decay, 0.0)
    t_inv = _unit_lower_inverse(m, eye)
    eg = jnp.exp(gc)
    sol = _bdot(t_inv.astype(BF16), jnp.concatenate([v * be, kb * eg], axis=2).astype(BF16))
    gc_last = gc[:, chunk - 1:chunk, :]

    def by_chunk(x):
        return x.reshape((nb, nc) + x.shape[1:])

    u_c = by_chunk(sol[:, :, 0:dv])
    wq = by_chunk(jnp.concatenate([sol[:, :, dv:dv + dk], q * eg], axis=1).astype(BF16))
    attn = by_chunk((kq[:, chunk:2 * chunk] * decay).astype(BF16))
    k_tail = by_chunk((k * jnp.exp(gc_last - gc)).astype(BF16))
    g_last = by_chunk(jnp.exp(gc_last))

    s = s_scr[...].reshape(nb, dk, dv)
    for c in range(nc):
        rows = slice(c * chunk, (c + 1) * chunk)
        ws = _bdot(wq[:, c], s.astype(BF16))
        v_new = (u_c[:, c] - ws[:, 0:chunk]).astype(BF16)
        o = ws[:, chunk:2 * chunk] + _bdot(attn[:, c], v_new)
        s = s * g_last[:, c] + _bdot_tn(k_tail[:, c], v_new)
        zg = jnp.stack([z_ref[:, rows, h * dv:(h + 1) * dv] for h in range(heads)], axis=1)
        og = (_rms(o, gn_ref[...]) * _silu(zg.reshape(nb, chunk, dv))).reshape(bb, heads, chunk, dv)
        for h in range(heads):
            o_ref[:, rows, h * dv:(h + 1) * dv] = og[:, h]
    s_scr[...] = s.reshape(bb, heads, dk, dv)

    conv_half()

    @pl.when((scan_t == nt - 1) & (s_id > 0))
    def _():
        sfin_ref[...] = s_scr[...]


def _gdn(qkv, z, ab, conv_buf, s0, conv_w, a_log_row, dt_bias_row, gn_w, layer, batch, seq, chunk_pref,
         bb):
    n, dconv = qkv.shape
    _, heads, dk, dv = s0.shape
    assert dk == LANES and dv == LANES
    cw = conv_w.shape[1]
    tb = _tile(seq, 256)
    chunk = _tile(tb, chunk_pref)
    nt = seq // tb
    dz = heads * dv
    assert batch % bb == 0
    n_tiles = (batch // bb) * nt

    def conv_tile(s):
        c = jnp.minimum(s, n_tiles - 1)
        return c // nt, c % nt

    def scan_tile(s):
        p = jnp.maximum(s - 1, 0)
        return p // nt, p % nt

    conv_rows = lambda s: (*conv_tile(s), 0)
    scan_rows = lambda s: (*scan_tile(s), 0)
    fixed = lambda s: (layer, 0, 0)
    og, s_new = pl.pallas_call(
        functools.partial(_gdn_kernel, bb=bb, tb=tb, chunk=chunk, heads=heads, dk=dk, dv=dv, cw=cw, nt=nt,
                          n_tiles=n_tiles),
        grid=(n_tiles + 1,),
        in_specs=[
            pl.BlockSpec((None, 1, LANES), fixed),
            pl.BlockSpec((None, 1, LANES), fixed),
            pl.BlockSpec((bb, tb, dconv), conv_rows),
            pl.BlockSpec((bb, tb, dz), scan_rows),
            pl.BlockSpec((bb, tb, LANES), conv_rows),
            pl.BlockSpec((bb, cw - 1, dconv), lambda s: (conv_tile(s)[0], 0, 0)),
            pl.BlockSpec((bb, heads, dk, dv), lambda s: (scan_tile(s)[0], 0, 0, 0)),
            pl.BlockSpec((None, cw, dconv), fixed),
            pl.BlockSpec((None, 1, dv), fixed),
        ],
        out_specs=[
            pl.BlockSpec((bb, tb, dz), scan_rows),
            pl.BlockSpec((bb, heads, dk, dv), lambda s: (scan_tile(s)[0], 0, 0, 0)),
        ],
        out_shape=[
            jax.ShapeDtypeStruct((batch, seq, dz), F32),
            jax.ShapeDtypeStruct((batch, heads, dk, dv), F32),
        ],
        scratch_shapes=[pltpu.VMEM((bb, CONV_PAD + tb, dconv), F32)]
        + [pltpu.VMEM((2, bb, heads, tb, LANES), F32)] * 6
        + [pltpu.VMEM((bb, heads, dk, dv), F32)],
        compiler_params=_cparams("arbitrary"),
        name="gdn",
    )(a_log_row, dt_bias_row, qkv.reshape(batch, seq, dconv), z.reshape(batch, seq, dz),
      ab.reshape(batch, seq, LANES), conv_buf, s0, conv_w, gn_w)
    return og.reshape(n, dz), s_new


def _out_proj_stage(x_ref, y_ref, w_ref):
    return x_ref[...] + _dot(y_ref[...].astype(BF16), w_ref[...])


def _pre_out_proj(x, y, w, layer):
    n, d = x.shape
    tm = _tile(n, FFN_ROWS)
    specs = [
        pl.BlockSpec((tm, d), lambda i: (i, 0)),
        pl.BlockSpec((tm, y.shape[1]), lambda i: (i, 0)),
        _resident((None, y.shape[1], d), lambda i: (layer, 0, 0)),
    ]
    return ([x, y, w], specs, _out_proj_stage, [])


def _trunk(x, batch, seq, pos0, cache, pool_bufs, conv_bufs, rec_states, w):
    depth = w["norm_mix"].shape[0]
    heads_a, dh = w["heads_a"], w["dh"]
    da = heads_a * dh
    new_k, new_v, new_pool, new_conv, new_rec = [], [], [], [], []
    for layer in range(depth):
        i = layer // 2
        n, d = x.shape
        x = _ffn(_pre_plain(x), n, d, w["norm_ffn"], w["w_ffn_in"], w["w_ffn_out"], layer, 0)
        if layer % 2 == 0:
            db_ = w["w_in_even"].shape[2] - 3 * da
            if cache is None:
                q, u, kt, vt, kt_bf, vt_bf = _proj_even_prompt(
                    x, w["norm_mix"], layer, w["w_in_even"], w["w_kv_even_t"], i, batch, seq, da)
                oa = _attn_prompt(q, kt_bf, vt_bf, w["sb_bias"][i][None], batch, seq, dh)
                mixed = _pre_even_prompt(x, oa, u, w["w_pool"], w["pool_scale"], w["w_out_even"], i, seq)
                new_pool.append(u.reshape(batch, seq, db_)[:, seq - (POOL_HALO - 1):])
                new_k.append(jnp.transpose(kt.reshape(batch, heads_a, dh, seq), (0, 3, 1, 2)))
                new_v.append(jnp.transpose(vt.reshape(batch, heads_a, dh, seq), (0, 3, 1, 2)))
            else:
                segs = [(0, da, (0,)), (da, 2 * da, (1,)), (2 * da, 3 * da, (2,)),
                        (3 * da, 3 * da + db_, (3,))]
                q, k, v, u = _proj(x, (w["norm_mix"], layer), (w["w_in_even"], i), segs, [F32] * 4,
                                   [da, da, da, db_], "in_even")
                new_k.append(k.reshape(batch, seq, heads_a, dh))
                new_v.append(v.reshape(batch, seq, heads_a, dh))
                cache_k, cache_v, page_table = cache
                oa = _attn_sample(q, k, v, cache_k, cache_v, i, page_table, w["sb_bias"][i][None], seq)
                u3 = u.reshape(batch, seq, db_)
                buf = pool_bufs[i]
                padded = jnp.concatenate(
                    [jnp.zeros((batch, POOL_HALO - buf.shape[1], db_), F32), buf, u3], axis=1)
                mixed = _pre_even_sample(x, oa, padded, w["w_pool"], w["pool_scale"], w["w_out_even"], i,
                                         seq, pos0)
                new_pool.append(padded[:, padded.shape[1] - buf.shape[1]:])
        else:
            heads_c, dk, dv = rec_states.shape[2:]
            dconv = heads_c * (2 * dk + dv)
            dz = heads_c * dv
            segs = [(0, dconv, (0,)), (dconv, dconv + dz, (1,)), (dconv + dz, dconv + dz + LANES, (2,))]
            qkv, z, ab = _proj(x, (w["norm_mix"], layer), (w["w_in_odd"], i), segs, [F32] * 3,
                               [dconv, dz, LANES], "in_odd")
            og, s_new = _gdn(qkv, z, ab, conv_bufs[i], rec_states[i], w["conv_w"], w["a_log"],
                             w["dt_bias"], w["gn_w"], i, batch, seq, 64, 1 if cache is None else 4)
            mixed = _pre_out_proj(x, og, w["w_out_odd"], i)
            tail = conv_bufs.shape[2]
            xp = jnp.concatenate([conv_bufs[i], qkv.reshape(batch, seq, dconv)], axis=1)
            new_conv.append(xp[:, xp.shape[1] - tail:])
            new_rec.append(s_new)
        final_g = w["norm_final"] if layer == depth - 1 else None
        x = _ffn(mixed, n, d, w["norm_ffn"], w["w_ffn_in"], w["w_ffn_out"], layer, 1, final_g)
    return x, jnp.stack(new_k), jnp.stack(new_v), jnp.stack(new_pool), jnp.stack(new_conv), jnp.stack(new_rec)


def kernel(x_prompt, x_sample, cache_k, cache_v, page_table, state_pool, state_conv, state_rec, norm_ffn,
           w_ffn_in, w_ffn_out, norm_mix, w_in_even, sb_bias, w_pool, pool_scale, w_out_even, w_in_odd,
           conv_w, a_log, dt_bias, gn_w, w_out_odd, norm_final):
    bp, seq, d = x_prompt.shape
    db, dseq, _ = x_sample.shape
    n_even, n_pool, page, heads_a, dh = cache_k.shape
    n_odd, _, heads_c, dk, dv = state_rec.shape
    da = heads_a * dh
    dconv = heads_c * (2 * dk + dv)
    dz = heads_c * dv
    depth = norm_mix.shape[0]

    def lane_row(p):
        return jnp.pad(p.astype(F32), ((0, 0), (0, LANES - p.shape[1])))[:, None, :]

    w_odd = jnp.concatenate(
        [w_in_odd, jnp.zeros((n_odd, d, LANES - 2 * heads_c), w_in_odd.dtype)], axis=2).astype(BF16)
    w = dict(
        heads_a=heads_a, dh=dh,
        norm_ffn=norm_ffn.reshape(depth, 2, 1, d),
        w_ffn_in=w_ffn_in.astype(BF16),
        w_ffn_out=w_ffn_out.astype(BF16),
        norm_mix=norm_mix.reshape(depth, 1, d),
        w_in_even=w_in_even.astype(BF16),
        w_kv_even_t=jnp.transpose(w_in_even[:, :, da:3 * da], (0, 2, 1)).astype(BF16),
        sb_bias=sb_bias.astype(F32),
        w_pool=w_pool.astype(BF16),
        pool_scale=pool_scale.reshape(n_even, 1, -1),
        w_out_even=w_out_even.astype(BF16),
        w_in_odd=w_odd,
        conv_w=conv_w,
        a_log=lane_row(a_log),
        dt_bias=lane_row(dt_bias),
        gn_w=gn_w.reshape(n_odd, 1, dv),
        w_out_odd=w_out_odd.astype(BF16),
        norm_final=norm_final.reshape(1, d),
    )
    past_len = page_table.shape[1] * page

    yp, kp, vp, poolp, convp, recp = _trunk(
        x_prompt.reshape(bp * seq, d), bp, seq, 0, None,
        None,
        jnp.zeros((n_odd, bp, conv_w.shape[1] - 1, dconv), state_conv.dtype),
        jnp.zeros((n_odd, bp, heads_c, dk, dv), state_rec.dtype), w)
    ys, ks, vs, pools, convs, recs = _trunk(
        x_sample.reshape(db * dseq, d), db, dseq, past_len, (cache_k, cache_v, page_table),
        state_pool, state_conv, state_rec, w)
    return (yp.reshape(bp, seq, d), ys.reshape(db, dseq, d), kp, vp, ks, vs, poolp, pools, convp, convs,
            recp, recs)
```

```python
import functools

import jax
import jax.numpy as jnp
from jax import lax
from jax.experimental import pallas as pl
from jax.experimental.pallas import tpu as pltpu

F32 = jnp.float32
BF16 = jnp.bfloat16
EPS = 1e-6
FFN_HALF = 0.5
POOL_WINDOWS = (2, 4, 8, 16)
POOL_HALO = 16
LANES = 128
SUBLANES = 8
VMEM_LIMIT = 56 * 1024 * 1024
NEG_LOG2E = -1.4426950408889634


def _cparams(*sem):
    return pltpu.CompilerParams(dimension_semantics=sem, vmem_limit_bytes=VMEM_LIMIT)


def _tile(n, pref):
    t = pref
    while t > SUBLANES and n % t:
        t //= 2
    assert n % t == 0, (n, pref)
    return t


def _dot(a, b):
    return jnp.dot(a, b, preferred_element_type=F32)


def _dot_nt(a, b):
    return lax.dot_general(a, b, (((1,), (1,)), ((), ())), preferred_element_type=F32)


def _rms(x, g):
    return x * lax.rsqrt(jnp.mean(x * x, axis=-1, keepdims=True) + EPS) * g


def _softplus(x):
    return jnp.maximum(x, 0.0) + jnp.log(1.0 + jnp.exp2(jnp.abs(x) * NEG_LOG2E))


def _silu(x):
    return x * jax.nn.sigmoid(x)


def _iota2(shape, dim):
    return lax.broadcasted_iota(jnp.int32, shape, dim)


def _resident(shape, index_map):
    return pl.BlockSpec(shape, index_map, pipeline_mode=pl.Buffered(1))


def _ffn_kernel(*refs, n_pre, n_scratch, pre_fn, d_ff, ck, final):
    pre_refs = refs[:n_pre]
    scratch = refs[len(refs) - n_scratch:]
    g_ref, wi_ref, wo_ref, *rest = refs[n_pre:len(refs) - n_scratch]
    o_ref = rest[-1]
    x = pre_fn(*pre_refs, *scratch)
    h = _rms(x, g_ref[...]).astype(BF16)
    acc = None
    for c in range(d_ff // ck):
        gate = _dot(h, wi_ref[:, c * ck:(c + 1) * ck])
        up = _dot(h, wi_ref[:, d_ff + c * ck:d_ff + (c + 1) * ck])
        act = (_silu(gate) * up).astype(BF16)
        part = _dot(act, wo_ref[c * ck:(c + 1) * ck, :])
        acc = part if acc is None else acc + part
    y = x + FFN_HALF * acc
    if final:
        y = _rms(y, rest[0][...])
    o_ref[...] = y


FFN_ROWS = 512


def _ffn(pre, n, d, norm_ffn, w_in, w_out, layer, j, final_g=None):
    pre_args, pre_specs, pre_fn, scratch_shapes = pre
    d_ff = w_out.shape[2]
    tm = _tile(n, FFN_ROWS)
    ck = 256 if d_ff % 256 == 0 else d_ff
    in_specs = list(pre_specs) + [
        pl.BlockSpec((None, None, 1, d), lambda i: (layer, j, 0, 0)),
        _resident((None, None, d, 2 * d_ff), lambda i: (layer, j, 0, 0)),
        _resident((None, None, d_ff, d), lambda i: (layer, j, 0, 0)),
    ]
    args = list(pre_args) + [norm_ffn, w_in, w_out]
    if final_g is not None:
        in_specs.append(pl.BlockSpec((1, d), lambda i: (0, 0)))
        args.append(final_g)
    return pl.pallas_call(
        functools.partial(_ffn_kernel, n_pre=len(pre_args), n_scratch=len(scratch_shapes), pre_fn=pre_fn,
                          d_ff=d_ff, ck=ck, final=final_g is not None),
        grid=(n // tm,),
        in_specs=in_specs,
        out_specs=pl.BlockSpec((tm, d), lambda i: (i, 0)),
        out_shape=jax.ShapeDtypeStruct((n, d), F32),
        scratch_shapes=list(scratch_shapes),
        compiler_params=_cparams("parallel"),
        name="ffn",
    )(*args)


def _pre_plain(x):
    n, d = x.shape
    tm = _tile(n, FFN_ROWS)
    return ([x], [pl.BlockSpec((tm, d), lambda i: (i, 0))], lambda x_ref: x_ref[...], [])


def _proj_kernel(x_ref, g_ref, w_ref, *o_refs, segs):
    h = _rms(x_ref[...], g_ref[...]).astype(BF16)
    for (a, b, outs) in segs:
        y = _dot(h, w_ref[:, a:b])
        for oi in outs:
            o_refs[oi][...] = y.astype(o_refs[oi].dtype)


def _proj(x, g, w, segs, out_dtypes, out_widths, name):
    n, d = x.shape
    tm = _tile(n, 512)
    g_arr, g_idx = g
    w_arr, w_idx = w
    in_specs = [
        pl.BlockSpec((tm, d), lambda i: (i, 0)),
        pl.BlockSpec((None, 1, d), lambda i: (g_idx, 0, 0)),
        _resident((None, d, w_arr.shape[2]), lambda i: (w_idx, 0, 0)),
    ]
    return pl.pallas_call(
        functools.partial(_proj_kernel, segs=segs),
        grid=(n // tm,),
        in_specs=in_specs,
        out_specs=[pl.BlockSpec((tm, wd), lambda i: (i, 0)) for wd in out_widths],
        out_shape=[jax.ShapeDtypeStruct((n, wd), dt) for wd, dt in zip(out_widths, out_dtypes)],
        compiler_params=_cparams("parallel"),
        name=name,
    )(x, g_arr, w_arr)


def _proj_even_prompt_kernel(x_ref, g_ref, w_ref, wkv_t_ref, q_ref, u_ref, kt_ref, vt_ref, ktb_ref, vtb_ref,
                             *, da):
    h = _rms(x_ref[...], g_ref[...]).astype(BF16)
    q_ref[...] = _dot(h, w_ref[:, 0:da])
    u_ref[...] = _dot(h, w_ref[:, 3 * da:])
    kv_t = _dot_nt(wkv_t_ref[...], h)
    kt_ref[...] = kv_t[0:da]
    vt_ref[...] = kv_t[da:2 * da]
    ktb_ref[...] = kv_t[0:da].astype(BF16)
    vtb_ref[...] = kv_t[da:2 * da].astype(BF16)


def _proj_even_prompt(x, norm_mix, layer, w, wkv_t, i, batch, seq, da):
    n, d = x.shape
    tm = _tile(seq, 512)
    per_seq = seq // tm
    db_ = w.shape[2] - 3 * da
    row_spec = lambda wd: pl.BlockSpec((tm, wd), lambda t: (t, 0))
    lane_spec = pl.BlockSpec((None, da, tm), lambda t: (t // per_seq, 0, t % per_seq))
    return pl.pallas_call(
        functools.partial(_proj_even_prompt_kernel, da=da),
        grid=(n // tm,),
        in_specs=[
            row_spec(d),
            pl.BlockSpec((None, 1, d), lambda t: (layer, 0, 0)),
            _resident((None, d, w.shape[2]), lambda t: (i, 0, 0)),
            _resident((None, 2 * da, d), lambda t: (i, 0, 0)),
        ],
        out_specs=[row_spec(da), row_spec(db_), lane_spec, lane_spec, lane_spec, lane_spec],
        out_shape=[
            jax.ShapeDtypeStruct((n, da), F32),
            jax.ShapeDtypeStruct((n, db_), F32),
            jax.ShapeDtypeStruct((batch, da, seq), F32),
            jax.ShapeDtypeStruct((batch, da, seq), F32),
            jax.ShapeDtypeStruct((batch, da, seq), BF16),
            jax.ShapeDtypeStruct((batch, da, seq), BF16),
        ],
        compiler_params=_cparams("parallel"),
        name="in_even_prompt",
    )(x, norm_mix, w, wkv_t)


def _suffix_matrix(n, terms):
    row = _iota2((terms * n, n), 0) % n
    col = _iota2((terms * n, n), 1)
    return jnp.where(row > col, 1.0, 0.0).astype(BF16)


ATTN_LANE_BLOCKS = 2


def _attn_prompt_kernel(bias_ref, q_ref, k_ref, v_ref, o_ref, r_scr, acc_scr, *, tq, dh, scale, groups):
    hg = pl.program_id(1)
    i = pl.program_id(2)
    nh = LANES // dh
    lane = _iota2((tq, LANES), 1)
    heads = [(lane >= e * dh) & (lane < (e + 1) * dh) for e in range(nh)]
    u_tri = _suffix_matrix(tq, 1)
    causal = _iota2((nh * tq, tq), 1) < (_iota2((nh * tq, tq), 0) % tq)
    qs, biases = [], []
    for g in range(groups):
        q = q_ref[:, g * LANES:(g + 1) * LANES] * scale
        qs.append(jnp.concatenate([jnp.where(hm, q, 0.0) for hm in heads], axis=0).astype(BF16))
        biases.append([bias_ref[0, (hg * groups + g) * nh + e] for e in range(nh)])

    def tile(j, valid):
        r0 = pl.multiple_of(j * tq, tq)
        gs = range(groups)
        zs = [_dot(qs[g], k_ref[g * LANES:(g + 1) * LANES, pl.ds(r0, tq)]) for g in gs]
        zs = [jnp.concatenate([zs[g][e * tq:(e + 1) * tq] + biases[g][e] for e in range(nh)], axis=0)
              for g in gs]
        sps = [_softplus(z) for z in zs]
        logsig = [z - sp for z, sp in zip(zs, sps)]
        drops = sps if valid is None else [jnp.where(valid, sp, 0.0) for sp in sps]
        laters = [_dot(d.astype(BF16), u_tri) for d in drops]
        for g in gs:
            a = jnp.exp(logsig[g] - laters[g] - r_scr[g])
            if valid is not None:
                a = jnp.where(valid, a, 0.0)
            acc_scr[g] += _dot_nt(a.astype(BF16), v_ref[g * LANES:(g + 1) * LANES, pl.ds(r0, tq)])
            r_scr[g] += jnp.sum(drops[g], axis=1, keepdims=True)

    r_scr[...] = jnp.zeros_like(r_scr)
    acc_scr[...] = jnp.zeros_like(acc_scr)
    tile(i, causal)

    def body(s, c):
        tile(i - 1 - s, None)
        return c

    lax.fori_loop(0, i, body, 0)
    for g in range(groups):
        out = jnp.where(heads[0], acc_scr[g, 0:tq], 0.0)
        for e in range(1, nh):
            out = out + jnp.where(heads[e], acc_scr[g, e * tq:(e + 1) * tq], 0.0)
        o_ref[:, g * LANES:(g + 1) * LANES] = out.astype(o_ref.dtype)


def _attn_prompt(q, k_bf, v_bf, sb_bias, batch, seq, dh):
    n, da = q.shape
    tq = _tile(seq, 256)
    nq = seq // tq
    groups = ATTN_LANE_BLOCKS if da % (ATTN_LANE_BLOCKS * LANES) == 0 else 1
    wl = groups * LANES
    rows = LANES // dh * tq
    return pl.pallas_call(
        functools.partial(_attn_prompt_kernel, tq=tq, dh=dh, scale=dh ** -0.5, groups=groups),
        grid=(batch, da // wl, nq),
        in_specs=[
            pl.BlockSpec(memory_space=pltpu.SMEM),
            pl.BlockSpec((tq, wl), lambda b, hg, i: (b * nq + i, hg)),
            pl.BlockSpec((None, wl, seq), lambda b, hg, i: (b, hg, 0)),
            pl.BlockSpec((None, wl, seq), lambda b, hg, i: (b, hg, 0)),
        ],
        out_specs=pl.BlockSpec((tq, wl), lambda b, hg, i: (b * nq + i, hg)),
        out_shape=jax.ShapeDtypeStruct((n, da), BF16),
        scratch_shapes=[pltpu.VMEM((groups, rows, 1), F32), pltpu.VMEM((groups, rows, LANES), F32)],
        compiler_params=_cparams("parallel", "parallel", "arbitrary"),
        name="attn_prompt",
    )(sb_bias, q, k_bf, v_bf)


def _attn_sample_kernel(pt_ref, bias_ref, q_ref, kn_ref, vn_ref, *rest, nseq, n_pages, page, t_new, heads,
                        dh, scale):
    del pt_ref
    k_pages = [rest[s * n_pages:(s + 1) * n_pages] for s in range(nseq)]
    v_pages = [rest[(nseq + s) * n_pages:(nseq + s + 1) * n_pages] for s in range(nseq)]
    o_ref, r_scr, acc_scr = rest[2 * nseq * n_pages:]
    j = pl.program_id(1)
    ht = heads * t_new
    seqs = range(nseq)
    hs = range(heads)
    q = q_ref[...] * scale
    qh = [[q[s * t_new:(s + 1) * t_new, h * dh:(h + 1) * dh].astype(BF16) for h in hs] for s in seqs]
    u_tri = _suffix_matrix(page, 2)

    def tile(logits, weighted, valid, slots):
        zs = [jnp.concatenate([logits(s, h) + bias_ref[0, h] for h in hs], axis=0) for s in seqs]
        sps = [_softplus(z) for z in zs]
        logsig = [z - sp for z, sp in zip(zs, sps)]
        drops = sps if valid is None else [jnp.where(valid, sp, 0.0) for sp in sps]
        his = [d.astype(BF16) for d in drops]
        los = [(d - hi.astype(F32)).astype(BF16) for d, hi in zip(drops, his)]
        cols = [slice(p * page, (p + 1) * page) for p in range(slots)]
        later = _dot(jnp.concatenate(
            [jnp.concatenate([his[s][:, c], los[s][:, c]], axis=1) for s in seqs for c in cols], axis=0),
            u_tri)
        for s in seqs:
            r_run = r_scr[s]
            shift = []
            for p, c in enumerate(cols):
                r0 = (s * slots + p) * ht
                shift.append(later[r0:r0 + ht] + r_run)
                r_run = r_run + jnp.sum(drops[s][:, c], axis=1, keepdims=True)
            r_scr[s] = r_run
            a = jnp.exp(logsig[s] - jnp.concatenate(shift, axis=1))
            if valid is not None:
                a = jnp.where(valid, a, 0.0)
            acc_scr[s] += jnp.concatenate(
                [weighted(s, h, a[h * t_new:(h + 1) * t_new].astype(BF16)) for h in hs], axis=0)

    @pl.when(j == 0)
    def _():
        r_scr[...] = jnp.zeros_like(r_scr)
        acc_scr[...] = jnp.zeros_like(acc_scr)
        pad = jnp.zeros((page - t_new, dh), F32)
        kn = kn_ref[...]
        vn = vn_ref[...]
        valid_new = _iota2((ht, page), 1) < (_iota2((ht, page), 0) % t_new)

        def padded(x, s, h):
            return jnp.concatenate([x[s * t_new:(s + 1) * t_new, h * dh:(h + 1) * dh], pad],
                                   axis=0).astype(BF16)

        tile(lambda s, h: _dot_nt(qh[s][h], padded(kn, s, h)), lambda s, h, a: _dot(a, padded(vn, s, h)),
             valid_new, 1)

    def lanes_of(refs, h):
        return jnp.concatenate([r[h] for r in refs], axis=1).astype(BF16)

    tile(lambda s, h: _dot(qh[s][h], lanes_of(k_pages[s], h)),
         lambda s, h, a: _dot_nt(a, lanes_of(v_pages[s], h)), None, n_pages)

    @pl.when(j == pl.num_programs(1) - 1)
    def _():
        for s in seqs:
            o_ref[s * t_new:(s + 1) * t_new, :] = jnp.concatenate(
                [acc_scr[s, h * t_new:(h + 1) * t_new, :] for h in hs], axis=1)


SAMPLE_PAGES_PER_STEP = 16
SAMPLE_SEQS_PER_STEP = 2


def _attn_sample(q, k_new, v_new, cache_k, cache_v, layer, page_table, sb_bias, t_new):
    n, da = q.shape
    db, n_pages = page_table.shape
    _, _, page, heads, dh = cache_k.shape
    cache_k = jnp.transpose(cache_k, (0, 1, 3, 4, 2))
    cache_v = jnp.transpose(cache_v, (0, 1, 3, 4, 2))
    pps = SAMPLE_PAGES_PER_STEP if n_pages % SAMPLE_PAGES_PER_STEP == 0 else n_pages
    steps = n_pages // pps
    nseq = SAMPLE_SEQS_PER_STEP if db % SAMPLE_SEQS_PER_STEP == 0 else 1
    page_specs = [
        pl.BlockSpec((None, None, heads, dh, page),
                     lambda b, j, pt, s=s, p=p: (layer, pt[b * nseq + s, n_pages - 1 - (j * pps + p)],
                                                 0, 0, 0))
        for s in range(nseq) for p in range(pps)
    ]
    row_spec = pl.BlockSpec((nseq * t_new, da), lambda b, j, pt: (b, 0))
    grid_spec = pltpu.PrefetchScalarGridSpec(
        num_scalar_prefetch=1,
        grid=(db // nseq, steps),
        in_specs=[pl.BlockSpec(memory_space=pltpu.SMEM), row_spec, row_spec, row_spec]
        + page_specs + page_specs,
        out_specs=row_spec,
        scratch_shapes=[pltpu.VMEM((nseq, heads * t_new, 1), F32),
                        pltpu.VMEM((nseq, heads * t_new, dh), F32)],
    )
    n_blocks = nseq * pps
    return pl.pallas_call(
        functools.partial(_attn_sample_kernel, nseq=nseq, n_pages=pps, page=page, t_new=t_new, heads=heads,
                          dh=dh, scale=dh ** -0.5),
        grid_spec=grid_spec,
        out_shape=jax.ShapeDtypeStruct((n, da), F32),
        compiler_params=_cparams("parallel", "arbitrary"),
        name="attn_sample",
    )(page_table, sb_bias, q, k_new, v_new, *([cache_k] * n_blocks), *([cache_v] * n_blocks))


def _even_out_tail(x, oa, diffs, wp_ref, ps_ref, wo_ref):
    mixed = [oa.astype(BF16)]
    for g, diff in enumerate(diffs):
        cb = diff.shape[1]
        ob = _dot(diff.astype(BF16), wp_ref[g]) * ps_ref[:, g * cb:(g + 1) * cb]
        mixed.append(ob.astype(BF16))
    return x + _dot(jnp.concatenate(mixed, axis=1), wo_ref[...])


def _even_out_prompt_stage(x_ref, oa_ref, u_ref, halo_ref, wp_ref, ps_ref, wo_ref, pad_scr, *,
                           tp, tiles_per_seq, cb):
    it = pl.program_id(0) % tiles_per_seq
    pad_scr[0:POOL_HALO, :] = jnp.where(it == 0, 0.0, halo_ref[...])
    pad_scr[POOL_HALO:POOL_HALO + tp, :] = u_ref[...]
    pos = it * tp + _iota2((tp, 1), 0)
    diffs = []
    for g, w in enumerate(POOL_WINDOWS):
        cols = slice(g * cb, (g + 1) * cb)
        s = pad_scr[POOL_HALO:POOL_HALO + tp, cols]
        for j in range(1, w):
            s = s + pad_scr[POOL_HALO - j:POOL_HALO - j + tp, cols]
        cnt = jnp.minimum(w, pos + 1).astype(F32)
        diffs.append(s / cnt - u_ref[:, cols])
    return _even_out_tail(x_ref[...], oa_ref[...], diffs, wp_ref, ps_ref, wo_ref)


def _pre_even_prompt(x, oa, u, w_pool, pool_scale, w_out, layer, seq):
    n, d = x.shape
    db_ = u.shape[1]
    tp = _tile(n, FFN_ROWS)
    assert seq % tp == 0 and tp % POOL_HALO == 0
    tiles_per_seq = seq // tp
    hb = tp // POOL_HALO
    g_b = w_pool.shape[1]
    cb = db_ // g_b
    specs = [
        pl.BlockSpec((tp, d), lambda i: (i, 0)),
        pl.BlockSpec((tp, oa.shape[1]), lambda i: (i, 0)),
        pl.BlockSpec((tp, db_), lambda i: (i, 0)),
        pl.BlockSpec((POOL_HALO, db_), lambda i: (jnp.maximum(i * hb - 1, 0), 0)),
        _resident((None, g_b, cb, cb), lambda i: (layer, 0, 0, 0)),
        pl.BlockSpec((None, 1, db_), lambda i: (layer, 0, 0)),
        _resident((None, d, d), lambda i: (layer, 0, 0)),
    ]
    fn = functools.partial(_even_out_prompt_stage, tp=tp, tiles_per_seq=tiles_per_seq, cb=cb)
    return ([x, oa, u, u, w_pool, pool_scale, w_out], specs, fn, [pltpu.VMEM((POOL_HALO + tp, db_), F32)])


def _even_out_sample_stage(x_ref, oa_ref, p_ref, wp_ref, ps_ref, wo_ref, *, bb, t_new, pos0, cb):
    pos = pos0 + _iota2((1, t_new, 1), 1)
    diffs = []
    for g, w in enumerate(POOL_WINDOWS):
        cols = slice(g * cb, (g + 1) * cb)
        u3 = p_ref[:, POOL_HALO:POOL_HALO + t_new, cols]
        s = u3
        for j in range(1, w):
            s = s + p_ref[:, POOL_HALO - j:POOL_HALO - j + t_new, cols]
        cnt = jnp.minimum(w, pos + 1).astype(F32)
        diffs.append((s / cnt - u3).reshape(bb * t_new, cb))
    return _even_out_tail(x_ref[...], oa_ref[...], diffs, wp_ref, ps_ref, wo_ref)


def _pre_even_sample(x, oa, padded, w_pool, pool_scale, w_out, layer, t_new, pos0):
    n, d = x.shape
    nb, rows, db_ = padded.shape
    tm = _tile(n, FFN_ROWS)
    assert tm % t_new == 0
    bb = tm // t_new
    g_b = w_pool.shape[1]
    cb = db_ // g_b
    specs = [
        pl.BlockSpec((tm, d), lambda i: (i, 0)),
        pl.BlockSpec((tm, oa.shape[1]), lambda i: (i, 0)),
        pl.BlockSpec((bb, rows, db_), lambda i: (i, 0, 0)),
        _resident((None, g_b, cb, cb), lambda i: (layer, 0, 0, 0)),
        pl.BlockSpec((None, 1, db_), lambda i: (layer, 0, 0)),
        _resident((None, d, d), lambda i: (layer, 0, 0)),
    ]
    fn = functools.partial(_even_out_sample_stage, bb=bb, t_new=t_new, pos0=pos0, cb=cb)
    return ([x, oa, padded, w_pool, pool_scale, w_out], specs, fn, [])


CONV_PAD = SUBLANES


def _bdot(a, b):
    return lax.dot_general(a, b, (((2,), (1,)), ((0,), (0,))), preferred_element_type=F32)


def _bdot_nt(a, b):
    return lax.dot_general(a, b, (((2,), (2,)), ((0,), (0,))), preferred_element_type=F32)


def _bdot_tn(a, b):
    return lax.dot_general(a, b, (((1,), (1,)), ((0,), (0,))), preferred_element_type=F32)


def _unit_lower_inverse(m, eye):
    n = m.shape[-1]
    p = eye - m
    x = m
    k = 2
    while k < n:
        xb = x.astype(BF16)
        x = _bdot(xb, xb)
        p = p + _bdot(p.astype(BF16), x.astype(BF16))
        k *= 2
    return p


def _gdn_kernel(alog_ref, dtb_ref, qkv_ref, z_ref, ab_ref, cbuf_ref, s0_ref, cw_ref, gn_ref,
                o_ref, sfin_ref, xpad, q_scr, k_scr, v_scr, gc_scr, df_scr, be_scr, s_scr, *,
                bb, tb, chunk, heads, dk, dv, cw, nt, n_tiles):
    s_id = pl.program_id(0)
    conv_t = jnp.minimum(s_id, n_tiles - 1) % nt
    scan_t = jnp.maximum(s_id - 1, 0) % nt
    w_slot = s_id % 2
    r_slot = (s_id + 1) % 2
    tail = cw - 1
    rows_all = bb * tb
    nb = bb * heads

    @pl.when(s_id == 0)
    def _():
        for ref in (q_scr, k_scr, v_scr, gc_scr, df_scr, be_scr):
            ref[1] = jnp.zeros(ref.shape[1:], F32)

    @pl.when(conv_t == 0)
    def _():
        xpad[:, CONV_PAD - tail:CONV_PAD, :] = cbuf_ref[...]

    @pl.when(scan_t == 0)
    def _():
        s_scr[...] = s0_ref[...]

    def conv_half():
        xpad[:, CONV_PAD:CONV_PAD + tb, :] = qkv_ref[...]

        def conv_silu(c0, width):
            acc = None
            for i in range(cw):
                r0 = CONV_PAD - tail + i
                term = xpad[:, r0:r0 + tb, c0:c0 + width] * cw_ref[i:i + 1, c0:c0 + width]
                acc = term if acc is None else acc + term
            return _silu(acc)

        def l2n(x):
            return x * lax.rsqrt(jnp.sum(x * x, axis=-1, keepdims=True) + EPS)

        for h in range(heads):
            q_scr[w_slot, :, h] = l2n(conv_silu(h * dk, dk)) * (dk ** -0.5)
            k_scr[w_slot, :, h] = l2n(conv_silu(heads * dk + h * dk, dk))
            v_scr[w_slot, :, h] = conv_silu(2 * heads * dk + h * dv, dv)

        xpad[:, CONV_PAD - tail:CONV_PAD, :] = xpad[:, CONV_PAD + tb - tail:CONV_PAD + tb, :]

        ab = ab_ref[...].reshape(rows_all, LANES)
        g = -jnp.exp(alog_ref[...]) * _softplus(ab + dtb_ref[...])
        beta = jax.nn.sigmoid(ab)
        rr = _iota2((rows_all, rows_all), 0)
        cc = _iota2((rows_all, rows_all), 1)
        l_blk = jnp.where((rr // chunk == cc // chunk) & (rr >= cc), 1.0, 0.0).astype(BF16)
        after = (_iota2((rows_all, LANES), 0) % chunk) > _iota2((rows_all, LANES), 1)
        for h in range(heads):
            g_rep = jnp.broadcast_to(g[:, h:h + 1], (rows_all, LANES))
            rhs = jnp.concatenate([g_rep, jnp.where(after, g_rep, 0.0)], axis=1)
            hi = rhs.astype(BF16)
            lo = (rhs - hi.astype(F32)).astype(BF16)
            res = _dot(l_blk, hi) + _dot(l_blk, lo)
            gc_scr[w_slot, :, h] = res[:, 0:LANES].reshape(bb, tb, LANES)
            df_scr[w_slot, :, h] = res[:, LANES:2 * LANES].reshape(bb, tb, LANES)
            be_scr[w_slot, :, h] = jnp.broadcast_to(
                beta[:, heads + h:heads + h + 1], (rows_all, LANES)).reshape(bb, tb, LANES)

    row = _iota2((chunk, chunk), 0)
    col = _iota2((chunk, chunk), 1)
    lower = row >= col
    strict = row > col
    eye = jnp.where(row == col, 1.0, 0.0)

    nc = tb // chunk

    def ld(ref):
        return ref[r_slot].reshape(nb * nc, chunk, LANES)

    q, k, v, gc, df, be = ld(q_scr), ld(k_scr), ld(v_scr), ld(gc_scr), ld(df_scr), ld(be_scr)
    kb = k * be
    decay = jnp.where(lower, jnp.exp(jnp.where(lower, df[:, :, 0:chunk], 0.0)), 0.0)
    kq = _bdot_nt(jnp.concatenate([kb, q], axis=1).astype(BF16), k.astype(BF16))
    m = jnp.where(strict, kq[:, 0:chunk] * decay, 0.0)
    t_inv = _unit_lower_inverse(m, eye)
    eg = jnp.exp(gc)
    sol = _bdot(t_inv.astype(BF16), jnp.concatenate([v * be, kb * eg], axis=2).astype(BF16))
    gc_last = gc[:, chunk - 1:chunk, :]

    def by_chunk(x):
        return x.reshape((nb, nc) + x.shape[1:])

    u_c = by_chunk(sol[:, :, 0:dv])
    wq = by_chunk(jnp.concatenate([sol[:, :, dv:dv + dk], q * eg], axis=1).astype(BF16))
    attn = by_chunk((kq[:, chunk:2 * chunk] * decay).astype(BF16))
    k_tail = by_chunk((k * jnp.exp(gc_last - gc)).astype(BF16))
    g_last = by_chunk(jnp.exp(gc_last))

    s = s_scr[...].reshape(nb, dk, dv)
    for c in range(nc):
        rows = slice(c * chunk, (c + 1) * chunk)
        ws = _bdot(wq[:, c], s.astype(BF16))
        v_new = (u_c[:, c] - ws[:, 0:chunk]).astype(BF16)
        o = ws[:, chunk:2 * chunk] + _bdot(attn[:, c], v_new)
        s = s * g_last[:, c] + _bdot_tn(k_tail[:, c], v_new)
        zg = jnp.stack([z_ref[:, rows, h * dv:(h + 1) * dv] for h in range(heads)], axis=1)
        og = (_rms(o, gn_ref[...]) * _silu(zg.reshape(nb, chunk, dv))).reshape(bb, heads, chunk, dv)
        for h in range(heads):
            o_ref[:, rows, h * dv:(h + 1) * dv] = og[:, h].astype(o_ref.dtype)
    s_scr[...] = s.reshape(bb, heads, dk, dv)

    conv_half()

    @pl.when((scan_t == nt - 1) & (s_id > 0))
    def _():
        sfin_ref[...] = s_scr[...]


def _gdn(qkv, z, ab, conv_buf, s0, conv_w, a_log_row, dt_bias_row, gn_w, layer, batch, seq, chunk_pref,
         bb):
    n, dconv = qkv.shape
    _, heads, dk, dv = s0.shape
    assert dk == LANES and dv == LANES
    cw = conv_w.shape[1]
    tb = _tile(seq, 256)
    chunk = _tile(tb, chunk_pref)
    nt = seq // tb
    dz = heads * dv
    assert batch % bb == 0
    n_tiles = (batch // bb) * nt

    def conv_tile(s):
        c = jnp.minimum(s, n_tiles - 1)
        return c // nt, c % nt

    def scan_tile(s):
        p = jnp.maximum(s - 1, 0)
        return p // nt, p % nt

    conv_rows = lambda s: (*conv_tile(s), 0)
    scan_rows = lambda s: (*scan_tile(s), 0)
    fixed = lambda s: (layer, 0, 0)
    og, s_new = pl.pallas_call(
        functools.partial(_gdn_kernel, bb=bb, tb=tb, chunk=chunk, heads=heads, dk=dk, dv=dv, cw=cw, nt=nt,
                          n_tiles=n_tiles),
        grid=(n_tiles + 1,),
        in_specs=[
            pl.BlockSpec((None, 1, LANES), fixed),
            pl.BlockSpec((None, 1, LANES), fixed),
            pl.BlockSpec((bb, tb, dconv), conv_rows),
            pl.BlockSpec((bb, tb, dz), scan_rows),
            pl.BlockSpec((bb, tb, LANES), conv_rows),
            pl.BlockSpec((bb, cw - 1, dconv), lambda s: (conv_tile(s)[0], 0, 0)),
            pl.BlockSpec((bb, heads, dk, dv), lambda s: (scan_tile(s)[0], 0, 0, 0)),
            pl.BlockSpec((None, cw, dconv), fixed),
            pl.BlockSpec((None, 1, dv), fixed),
        ],
        out_specs=[
            pl.BlockSpec((bb, tb, dz), scan_rows),
            pl.BlockSpec((bb, heads, dk, dv), lambda s: (scan_tile(s)[0], 0, 0, 0)),
        ],
        out_shape=[
            jax.ShapeDtypeStruct((batch, seq, dz), BF16 if chunk % (2 * SUBLANES) == 0 else F32),
            jax.ShapeDtypeStruct((batch, heads, dk, dv), F32),
        ],
        scratch_shapes=[pltpu.VMEM((bb, CONV_PAD + tb, dconv), F32)]
        + [pltpu.VMEM((2, bb, heads, tb, LANES), F32)] * 6
        + [pltpu.VMEM((bb, heads, dk, dv), F32)],
        compiler_params=_cparams("arbitrary"),
        name="gdn",
    )(a_log_row, dt_bias_row, qkv.reshape(batch, seq, dconv), z.reshape(batch, seq, dz),
      ab.reshape(batch, seq, LANES), conv_buf, s0, conv_w, gn_w)
    return og.reshape(n, dz), s_new


def _out_proj_stage(x_ref, y_ref, w_ref):
    return x_ref[...] + _dot(y_ref[...].astype(BF16), w_ref[...])


def _pre_out_proj(x, y, w, layer):
    n, d = x.shape
    tm = _tile(n, FFN_ROWS)
    specs = [
        pl.BlockSpec((tm, d), lambda i: (i, 0)),
        pl.BlockSpec((tm, y.shape[1]), lambda i: (i, 0)),
        _resident((None, y.shape[1], d), lambda i: (layer, 0, 0)),
    ]
    return ([x, y, w], specs, _out_proj_stage, [])


def _trunk(x, batch, seq, pos0, cache, pool_bufs, conv_bufs, rec_states, w):
    depth = w["norm_mix"].shape[0]
    heads_a, dh = w["heads_a"], w["dh"]
    da = heads_a * dh
    new_k, new_v, new_pool, new_conv, new_rec = [], [], [], [], []
    for layer in range(depth):
        i = layer // 2
        n, d = x.shape
        x = _ffn(_pre_plain(x), n, d, w["norm_ffn"], w["w_ffn_in"], w["w_ffn_out"], layer, 0)
        if layer % 2 == 0:
            db_ = w["w_in_even"].shape[2] - 3 * da
            if cache is None:
                q, u, kt, vt, kt_bf, vt_bf = _proj_even_prompt(
                    x, w["norm_mix"], layer, w["w_in_even"], w["w_kv_even_t"], i, batch, seq, da)
                oa = _attn_prompt(q, kt_bf, vt_bf, w["sb_bias"][i][None], batch, seq, dh)
                mixed = _pre_even_prompt(x, oa, u, w["w_pool"], w["pool_scale"], w["w_out_even"], i, seq)
                new_pool.append(u.reshape(batch, seq, db_)[:, seq - (POOL_HALO - 1):])
                new_k.append(jnp.transpose(kt.reshape(batch, heads_a, dh, seq), (0, 3, 1, 2)))
                new_v.append(jnp.transpose(vt.reshape(batch, heads_a, dh, seq), (0, 3, 1, 2)))
            else:
                segs = [(0, da, (0,)), (da, 2 * da, (1,)), (2 * da, 3 * da, (2,)),
                        (3 * da, 3 * da + db_, (3,))]
                q, k, v, u = _proj(x, (w["norm_mix"], layer), (w["w_in_even"], i), segs, [F32] * 4,
                                   [da, da, da, db_], "in_even")
                new_k.append(k.reshape(batch, seq, heads_a, dh))
                new_v.append(v.reshape(batch, seq, heads_a, dh))
                cache_k, cache_v, page_table = cache
                oa = _attn_sample(q, k, v, cache_k, cache_v, i, page_table, w["sb_bias"][i][None], seq)
                u3 = u.reshape(batch, seq, db_)
                buf = pool_bufs[i]
                padded = jnp.concatenate(
                    [jnp.zeros((batch, POOL_HALO - buf.shape[1], db_), F32), buf, u3], axis=1)
                mixed = _pre_even_sample(x, oa, padded, w["w_pool"], w["pool_scale"], w["w_out_even"], i,
                                         seq, pos0)
                new_pool.append(padded[:, padded.shape[1] - buf.shape[1]:])
        else:
            heads_c, dk, dv = rec_states.shape[2:]
            dconv = heads_c * (2 * dk + dv)
            dz = heads_c * dv
            segs = [(0, dconv, (0,)), (dconv, dconv + dz, (1,)), (dconv + dz, dconv + dz + LANES, (2,))]
            qkv, z, ab = _proj(x, (w["norm_mix"], layer), (w["w_in_odd"], i), segs, [F32] * 3,
                               [dconv, dz, LANES], "in_odd")
            og, s_new = _gdn(qkv, z, ab, conv_bufs[i], rec_states[i], w["conv_w"], w["a_log"],
                             w["dt_bias"], w["gn_w"], i, batch, seq, 64, 1 if cache is None else 4)
            mixed = _pre_out_proj(x, og, w["w_out_odd"], i)
            tail = conv_bufs.shape[2]
            xp = jnp.concatenate([conv_bufs[i], qkv.reshape(batch, seq, dconv)], axis=1)
            new_conv.append(xp[:, xp.shape[1] - tail:])
            new_rec.append(s_new)
        final_g = w["norm_final"] if layer == depth - 1 else None
        x = _ffn(mixed, n, d, w["norm_ffn"], w["w_ffn_in"], w["w_ffn_out"], layer, 1, final_g)
    return x, jnp.stack(new_k), jnp.stack(new_v), jnp.stack(new_pool), jnp.stack(new_conv), jnp.stack(new_rec)


def kernel(x_prompt, x_sample, cache_k, cache_v, page_table, state_pool, state_conv, state_rec, norm_ffn,
           w_ffn_in, w_ffn_out, norm_mix, w_in_even, sb_bias, w_pool, pool_scale, w_out_even, w_in_odd,
           conv_w, a_log, dt_bias, gn_w, w_out_odd, norm_final):
    bp, seq, d = x_prompt.shape
    db, dseq, _ = x_sample.shape
    n_even, n_pool, page, heads_a, dh = cache_k.shape
    n_odd, _, heads_c, dk, dv = state_rec.shape
    da = heads_a * dh
    dconv = heads_c * (2 * dk + dv)
    dz = heads_c * dv
    depth = norm_mix.shape[0]

    def lane_row(p):
        return jnp.pad(p.astype(F32), ((0, 0), (0, LANES - p.shape[1])))[:, None, :]

    w_odd = jnp.concatenate(
        [w_in_odd, jnp.zeros((n_odd, d, LANES - 2 * heads_c), w_in_odd.dtype)], axis=2).astype(BF16)
    w = dict(
        heads_a=heads_a, dh=dh,
        norm_ffn=norm_ffn.reshape(depth, 2, 1, d),
        w_ffn_in=w_ffn_in.astype(BF16),
        w_ffn_out=w_ffn_out.astype(BF16),
        norm_mix=norm_mix.reshape(depth, 1, d),
        w_in_even=w_in_even.astype(BF16),
        w_kv_even_t=jnp.transpose(w_in_even[:, :, da:3 * da], (0, 2, 1)).astype(BF16),
        sb_bias=sb_bias.astype(F32),
        w_pool=w_pool.astype(BF16),
        pool_scale=pool_scale.reshape(n_even, 1, -1),
        w_out_even=w_out_even.astype(BF16),
        w_in_odd=w_odd,
        conv_w=conv_w,
        a_log=lane_row(a_log),
        dt_bias=lane_row(dt_bias),
        gn_w=gn_w.reshape(n_odd, 1, dv),
        w_out_odd=w_out_odd.astype(BF16),
        norm_final=norm_final.reshape(1, d),
    )
    past_len = page_table.shape[1] * page

    yp, kp, vp, poolp, convp, recp = _trunk(
        x_prompt.reshape(bp * seq, d), bp, seq, 0, None,
        None,
        jnp.zeros((n_odd, bp, conv_w.shape[1] - 1, dconv), state_conv.dtype),
        jnp.zeros((n_odd, bp, heads_c, dk, dv), state_rec.dtype), w)
    ys, ks, vs, pools, convs, recs = _trunk(
        x_sample.reshape(db * dseq, d), db, dseq, past_len, (cache_k, cache_v, page_table),
        state_pool, state_conv, state_rec, w)
    return (yp.reshape(bp, seq, d), ys.reshape(db, dseq, d), kp, vp, ks, vs, poolp, pools, convp, convs,
            recp, recs)
```
